```python
import math
import functools
import jax
import jax.numpy as jnp
from jax import lax
import numpy as np

D_MODEL = 1024
BATCH = 4
SEQ = 4096
DEPTH = 1
DEC_BATCH = 128
DEC_SEQ = 4
PAST_LEN = 2048
PAGE_SIZE = 128

HEAD_DIM = 64
ATT_BRANCHES = ((128, 1), (512, 4), (2048, 16))
N_BRANCH = len(ATT_BRANCHES)
HEADS_PER_BRANCH = 4
ATT_HEADS = N_BRANCH * HEADS_PER_BRANCH
D_ATT = ATT_HEADS * HEAD_DIM
SSD_HEADS = 12
SSD_HEAD_DIM = 64
D_INNER = SSD_HEADS * SSD_HEAD_DIM
SSD_GROUPS = 4
SSD_HEADS_PER_GROUP = SSD_HEADS // SSD_GROUPS
SSD_STATE = 128
CONV_WIDTH = 4
CONV_DIM = D_INNER + 2 * SSD_GROUPS * SSD_STATE
SSD_CHUNK = 128
D_MIX = D_ATT + D_INNER
D_IN_TOTAL = 3 * D_ATT + D_INNER + CONV_DIM + SSD_HEADS
D_FF = ((8 * D_MODEL + 3 * 256 - 1) // (3 * 256)) * 256
RMS_EPS = 1e-5
NEG_INF = -1e30
ATT_SCALE = HEAD_DIM ** -0.5

kernel_name = 'hybrid_dilated_attn_ssd_step'


def rmsnorm(x, g):
    xf = x.astype(jnp.float32)
    inv = lax.rsqrt(jnp.mean(xf * xf, axis=-1, keepdims=True) + RMS_EPS)
    return (xf * inv * g.astype(jnp.float32)).astype(x.dtype)


def project(x, norm_mix, w_in):
    b, t = x.shape[:2]
    u = rmsnorm(x, norm_mix) @ w_in
    q, k, v = (u[..., i * D_ATT:(i + 1) * D_ATT].reshape(b, t, N_BRANCH, HEADS_PER_BRANCH, HEAD_DIM)
               for i in range(3))
    o = 3 * D_ATT
    z = u[..., o:o + D_INNER]
    o += D_INNER
    xbc = u[..., o:o + CONV_DIM]
    o += CONV_DIM
    dt_raw = u[..., o:o + SSD_HEADS]
    return q, k, v, z, xbc, dt_raw


def dilated_branch_prompt(q, k, v, window, dil):
    b, t, h, d = q.shape
    band = window // dil
    unit = band * dil
    tp = -(-t // unit) * unit
    nblk = tp // unit

    def to_blocks(a):
        a = jnp.pad(a.astype(jnp.float32), ((0, 0), (0, tp - t), (0, 0), (0, 0)))
        a = a.reshape(b, tp // dil, dil, h, d).transpose(0, 2, 1, 3, 4)
        return a.reshape(b, dil, nblk, band, h, d)

    def with_prev(a):
        prev = jnp.pad(a[:, :, :-1], ((0, 0), (0, 0), (1, 0), (0, 0), (0, 0), (0, 0)))
        return jnp.concatenate([prev, a], axis=3)

    qb = to_blocks(q)
    kk = with_prev(to_blocks(k))
    vv = with_prev(to_blocks(v))
    s = jnp.einsum('brnqhd,brnkhd->brnhqk', qb, kk) * ATT_SCALE
    qi = jnp.arange(band)[:, None]
    kj = jnp.arange(2 * band)[None, :]
    dist = band + qi - kj
    in_band = (dist >= 0) & (dist <= band)
    has_prev = (jnp.arange(nblk)[:, None, None] > 0) | (kj[None] >= band)
    valid = in_band[None] & has_prev
    s = jnp.where(valid[None, None, :, None], s, NEG_INF)
    m = jnp.max(s, axis=-1, keepdims=True)
    p = jnp.exp(s - m)
    den = jnp.sum(p, axis=-1)
    o = jnp.einsum('brnhqk,brnkhd->brnqhd', p, vv) / den.transpose(0, 1, 2, 4, 3)[..., None]
    lse = (m[..., 0] + jnp.log(den)).transpose(0, 1, 2, 4, 3)
    o = o.reshape(b, dil, tp // dil, h, d).transpose(0, 2, 1, 3, 4).reshape(b, tp, h, d)[:, :t]
    lse = lse.reshape(b, dil, tp // dil, h).transpose(0, 2, 1, 3).reshape(b, tp, h)[:, :t]
    return o, lse


def dilated_branch_sample(q, k_all, v_all, window, dil, lb):
    s_len = q.shape[1]
    band = window // dil
    idx = lb + jnp.arange(s_len)[:, None] - dil * jnp.arange(band + 1)[None, :]
    valid = idx >= 0
    idx = jnp.maximum(idx, 0)
    kg = k_all[:, idx].astype(jnp.float32)
    vg = v_all[:, idx].astype(jnp.float32)
    sc = jnp.einsum('bshd,bsjhd->bhsj', q.astype(jnp.float32), kg) * ATT_SCALE
    sc = jnp.where(valid[None, None], sc, NEG_INF)
    m = jnp.max(sc, axis=-1, keepdims=True)
    p = jnp.exp(sc - m)
    den = jnp.sum(p, axis=-1)
    o = jnp.einsum('bhsj,bsjhd->bshd', p, vg) / den.transpose(0, 2, 1)[..., None]
    lse = (m[..., 0] + jnp.log(den)).transpose(0, 2, 1)
    return o, lse


def merge_branches(outs, lses):
    o = jnp.stack(outs, axis=2)
    alpha = jax.nn.softmax(jnp.stack(lses, axis=2), axis=2)
    b, t = o.shape[:2]
    return (o * alpha[..., None]).reshape(b, t, D_ATT)


def attend_prompt(q, k, v):
    t = q.shape[1]
    outs, lses, kv_rows = [], [], []
    for g, (win, dil) in enumerate(ATT_BRANCHES):
        o, l = dilated_branch_prompt(q[:, :, g], k[:, :, g], v[:, :, g], win, dil)
        outs.append(o)
        lses.append(l)
        kv_rows.append(jnp.stack([k[:, :, g], v[:, :, g]], axis=2)[:, t - min(win, t):])
    return merge_branches(outs, lses), kv_rows


def attend_sample(q, k, v, kv_caches):
    s_len = q.shape[1]
    outs, lses, kv_rows = [], [], []
    for g, (win, dil) in enumerate(ATT_BRANCHES):
        cache = kv_caches[g]
        lb = cache.shape[1]
        new_rows = jnp.stack([k[:, :, g], v[:, :, g]], axis=2).astype(cache.dtype)
        kv_all = jnp.concatenate([cache, new_rows], axis=1)
        o, l = dilated_branch_sample(q[:, :, g], kv_all[:, :, 0], kv_all[:, :, 1], win, dil, lb)
        outs.append(o)
        lses.append(l)
        keep = min(win, lb + s_len)
        kv_rows.append(kv_all[:, lb + s_len - keep:])
    return merge_branches(outs, lses), kv_rows


def ssd_scan(xs, dt, a, bm, cm, h0):
    b, t = xs.shape[:2]
    cl = math.gcd(SSD_CHUNK, t)
    nc = t // cl
    G, R, P, N = SSD_GROUPS, SSD_HEADS_PER_GROUP, SSD_HEAD_DIM, SSD_STATE
    x = xs.reshape(b, nc, cl, G, R, P)
    dtc = dt.reshape(b, nc, cl, G, R)
    bc = bm.reshape(b, nc, cl, G, N)
    cc = cm.reshape(b, nc, cl, G, N)
    acum = jnp.cumsum(dtc * a.reshape(G, R), axis=2)
    seg = acum[:, :, :, None] - acum[:, :, None]
    causal = jnp.tril(jnp.ones((cl, cl), dtype=bool))[:, :, None, None]
    lmat = jnp.exp(jnp.where(causal, seg, -jnp.inf))
    cb = jnp.einsum('bclgn,bcsgn->bclsg', cc, bc)
    y_diag = jnp.einsum('bclsgr,bcsgrp->bclgrp', cb[..., None] * lmat * dtc[:, :, None], x)
    decay = jnp.exp(acum[:, :, -1:] - acum)
    states = jnp.einsum('bclgn,bclgr,bclgrp->bcgrpn', bc, decay * dtc, x)
    chunk_decay = jnp.exp(acum[:, :, -1])

    def step(h, inp):
        st, dec = inp
        return dec[..., None, None] * h + st, h

    h_last, h_prev = lax.scan(step, h0.reshape(b, G, R, P, N),
                              (states.transpose(1, 0, 2, 3, 4, 5), chunk_decay.transpose(1, 0, 2, 3)))
    h_prev = h_prev.transpose(1, 0, 2, 3, 4, 5)
    y_off = jnp.einsum('bclgn,bcgrpn,bclgr->bclgrp', cc, h_prev, jnp.exp(acum))
    y = (y_diag + y_off).reshape(b, t, SSD_HEADS, P)
    return y, h_last.reshape(b, SSD_HEADS, P, N)


def ssd_mixer(z, xbc, dt_raw, conv_prev, ssm_prev, conv_w, conv_b, dt_bias, a_log, d_skip, norm_ssd):
    b, t = xbc.shape[:2]
    f32 = jnp.float32
    xpad = jnp.concatenate([conv_prev, xbc.astype(conv_prev.dtype)], axis=1)
    new_conv = xpad[:, -(CONV_WIDTH - 1):]
    xc = lax.conv_general_dilated(xpad.astype(f32), conv_w.astype(f32)[:, None, :], (1,), 'VALID',
                                  dimension_numbers=('NWC', 'WIO', 'NWC'),
                                  feature_group_count=CONV_DIM) + conv_b.astype(f32)
    xc = jax.nn.silu(xc)
    gn = SSD_GROUPS * SSD_STATE
    xs = xc[..., :D_INNER].reshape(b, t, SSD_HEADS, SSD_HEAD_DIM)
    bm = xc[..., D_INNER:D_INNER + gn].reshape(b, t, SSD_GROUPS, SSD_STATE)
    cm = xc[..., D_INNER + gn:].reshape(b, t, SSD_GROUPS, SSD_STATE)
    dt = jax.nn.softplus(dt_raw.astype(f32) + dt_bias.astype(f32))
    a = -jnp.exp(a_log.astype(f32))
    y, h_last = ssd_scan(xs, dt, a, bm, cm, ssm_prev.astype(f32))
    y = y + d_skip.astype(f32)[:, None] * xs
    y = y.reshape(b, t, D_INNER) * jax.nn.silu(z.astype(f32))
    y = rmsnorm(y, norm_ssd)
    return y, new_conv, h_last.astype(ssm_prev.dtype)


def decoder_layer(x, attend, conv_prev, ssm_prev, norm_mix, w_in, conv_w, conv_b, dt_bias, a_log,
                  d_skip, norm_ssd, w_out, norm_ffn, w_gate_up, w_down):
    q, k, v, z, xbc, dt_raw = project(x, norm_mix, w_in)
    att, kv_rows = attend(q, k, v)
    ssd, conv_new, ssm_new = ssd_mixer(z, xbc, dt_raw, conv_prev, ssm_prev, conv_w, conv_b,
                                       dt_bias, a_log, d_skip, norm_ssd)
    mixed = jnp.concatenate([att, ssd], axis=-1).astype(x.dtype)
    h = x + mixed @ w_out
    gu = rmsnorm(h, norm_ffn) @ w_gate_up
    y = h + (jax.nn.silu(gu[..., :D_FF]) * gu[..., D_FF:]) @ w_down
    return y, kv_rows, conv_new, ssm_new


def setup_inputs(seed: int = 0) -> dict:
    key = jax.random.key(seed)
    ks = jax.random.split(key, 20)
    f32 = jnp.float32
    nrm = lambda k, shp, sc=1.0: (jax.random.normal(k, shp, f32) * sc).astype(f32)
    caches = {}
    for i, (win, dil) in enumerate(ATT_BRANCHES):
        caches['cache_kv_d%d' % dil] = nrm(ks[2 + i], (DEPTH, DEC_BATCH, min(win, PAST_LEN), 2,
                                                     HEADS_PER_BRANCH, HEAD_DIM))
    dt0 = jnp.exp(jax.random.uniform(ks[8], (DEPTH, SSD_HEADS), f32,
                                     minval=math.log(1e-3), maxval=math.log(1e-1)))
    return {
        'x_prompt': nrm(ks[0], (BATCH, SEQ, D_MODEL)),
        'x_sample': nrm(ks[1], (DEC_BATCH, DEC_SEQ, D_MODEL)),
        'cache_kv_d1': caches['cache_kv_d1'],
        'cache_kv_d4': caches['cache_kv_d4'],
        'cache_kv_d16': caches['cache_kv_d16'],
        'state_conv': nrm(ks[5], (DEPTH, DEC_BATCH, CONV_WIDTH - 1, CONV_DIM)),
        'state_ssm': nrm(ks[6], (DEPTH, DEC_BATCH, SSD_HEADS, SSD_HEAD_DIM, SSD_STATE), 0.1),
        'norm_mix': 1.0 + nrm(ks[7], (DEPTH, D_MODEL), 0.02),
        'w_in': nrm(ks[9], (DEPTH, D_MODEL, D_IN_TOTAL), D_MODEL ** -0.5),
        'conv_w': nrm(ks[10], (DEPTH, CONV_WIDTH, CONV_DIM), CONV_WIDTH ** -0.5),
        'conv_b': nrm(ks[11], (DEPTH, CONV_DIM), 0.02),
        'dt_bias': dt0 + jnp.log(-jnp.expm1(-dt0)),
        'a_log': jnp.log(jax.random.uniform(ks[12], (DEPTH, SSD_HEADS), f32, minval=1.0, maxval=16.0)),
        'd_skip': 1.0 + nrm(ks[13], (DEPTH, SSD_HEADS), 0.1),
        'norm_ssd': 1.0 + nrm(ks[14], (DEPTH, D_INNER), 0.02),
        'w_out': nrm(ks[15], (DEPTH, D_MIX, D_MODEL), D_MIX ** -0.5),
        'norm_ffn': 1.0 + nrm(ks[16], (DEPTH, D_MODEL), 0.02),
        'w_gate_up': nrm(ks[17], (DEPTH, D_MODEL, 2 * D_FF), D_MODEL ** -0.5),
        'w_down': nrm(ks[18], (DEPTH, D_FF, D_MODEL), D_FF ** -0.5),
        'norm_final': 1.0 + nrm(ks[19], (D_MODEL,), 0.02),
    }


def reference(x_prompt, x_sample, cache_kv_d1, cache_kv_d4, cache_kv_d16, state_conv, state_ssm,
              norm_mix, w_in, conv_w, conv_b, dt_bias, a_log, d_skip, norm_ssd, w_out, norm_ffn,
              w_gate_up, w_down, norm_final):
    hp, hs = x_prompt, x_sample
    p_new = [[] for _ in range(5)]
    s_new = [[] for _ in range(5)]
    for l in range(DEPTH):
        w = (norm_mix[l], w_in[l], conv_w[l], conv_b[l], dt_bias[l], a_log[l], d_skip[l],
             norm_ssd[l], w_out[l], norm_ffn[l], w_gate_up[l], w_down[l])
        conv0 = jnp.zeros((hp.shape[0], CONV_WIDTH - 1, CONV_DIM), hp.dtype)
        ssm0 = jnp.zeros((hp.shape[0], SSD_HEADS, SSD_HEAD_DIM, SSD_STATE), hp.dtype)
        hp, kv_p, conv_p, ssm_p = decoder_layer(hp, attend_prompt, conv0, ssm0, *w)
        caches_l = (cache_kv_d1[l], cache_kv_d4[l], cache_kv_d16[l])
        hs, kv_s, conv_s, ssm_s = decoder_layer(hs, functools.partial(attend_sample, kv_caches=caches_l),
                                                state_conv[l], state_ssm[l], *w)
        for i, arr in enumerate((kv_p[0], kv_p[1], kv_p[2], conv_p, ssm_p)):
            p_new[i].append(arr)
        for i, arr in enumerate((kv_s[0], kv_s[1], kv_s[2], conv_s, ssm_s)):
            s_new[i].append(arr)
    y_prompt = rmsnorm(hp, norm_final)
    y_sample = rmsnorm(hs, norm_final)
    p_kv_d1, p_kv_d4, p_kv_d16, p_conv, p_ssm = [jnp.stack(a) for a in p_new]
    s_kv_d1, s_kv_d4, s_kv_d16, s_conv, s_ssm = [jnp.stack(a) for a in s_new]
    return (y_prompt, y_sample, p_kv_d1, p_kv_d4, p_kv_d16, p_conv, p_ssm,
            s_kv_d1, s_kv_d4, s_kv_d16, s_conv, s_ssm)
```

```python
import functools

import jax
import jax.numpy as jnp
from jax import lax
from jax.experimental import pallas as pl
from jax.experimental.pallas import tpu as pltpu

F32 = jnp.float32
BF16 = jnp.bfloat16

D_MODEL = 1024
HEAD_DIM = 64
ATT_BRANCHES = ((128, 1), (512, 4), (2048, 16))
N_BRANCH = 3
HEADS_PER_BRANCH = 4
D_BRANCH = HEADS_PER_BRANCH * HEAD_DIM
D_ATT = N_BRANCH * D_BRANCH
SSD_HEADS = 12
SSD_HEAD_DIM = 64
D_INNER = SSD_HEADS * SSD_HEAD_DIM
SSD_GROUPS = 4
SSD_HEADS_PER_GROUP = 3
SSD_STATE = 128
CONV_WIDTH = 4
CONV_DIM = D_INNER + 2 * SSD_GROUPS * SSD_STATE
SSD_CHUNK = 128
D_MIX = D_ATT + D_INNER
D_FF = 2816
RMS_EPS = 1e-5
NEG_INF = -1e30
ATT_SCALE = HEAD_DIM ** -0.5
BAND = 128

LANES = 128
SUBLANES = 8
KV_W = 2 * D_BRANCH
DT_PAD = LANES
FF_CHUNK = 256
PROJ_CHUNK = 256

_NT = (((1,), (1,)), ((), ()))


def _rms(x, g):
    inv = lax.rsqrt(jnp.mean(x * x, axis=-1, keepdims=True) + RMS_EPS)
    return x * inv * g


def _sigmoid(x):
    return 1.0 / (1.0 + jnp.exp(-x))


def _softplus(x):
    return jnp.maximum(x, 0.0) + jnp.log1p(jnp.exp(-jnp.abs(x)))


def _const_spec(shape):
    nd = len(shape)
    return pl.BlockSpec(shape, lambda *_: (0,) * nd, pipeline_mode=pl.Buffered(1))


_IN_SEGS = (D_ATT, KV_W, KV_W, KV_W, D_INNER, CONV_DIM, DT_PAD)
_IN_TOTAL = sum(_IN_SEGS)


def _in_proj_kernel(x_ref, g_ref, w_ref, q_ref, kv1_ref, kv4_ref, kv16_ref, z_ref, xbc_ref, dt_ref):
    xn = _rms(x_ref[...], g_ref[...]).astype(BF16)
    outs = (q_ref, kv1_ref, kv4_ref, kv16_ref, z_ref, xbc_ref, dt_ref)
    col = 0
    for ref, width in zip(outs, _IN_SEGS):
        for c in range(0, width, PROJ_CHUNK):
            w = min(PROJ_CHUNK, width - c)
            acc = jnp.dot(xn, w_ref[:, col + c:col + c + w], preferred_element_type=F32)
            if ref is q_ref:
                acc = acc * ATT_SCALE
            ref[:, c:c + w] = acc.astype(ref.dtype)
        col += width


def _in_proj(x2d, g, w_perm, tm):
    m = x2d.shape[0]
    out_dtypes = (BF16, F32, F32, F32, F32, F32, F32)
    return pl.pallas_call(
        _in_proj_kernel,
        grid=(m // tm,),
        in_specs=[
            pl.BlockSpec((tm, D_MODEL), lambda i: (i, 0)),
            _const_spec((1, D_MODEL)),
            _const_spec((D_MODEL, _IN_TOTAL)),
        ],
        out_specs=[pl.BlockSpec((tm, w), lambda i: (i, 0)) for w in _IN_SEGS],
        out_shape=[jax.ShapeDtypeStruct((m, w), dt) for w, dt in zip(_IN_SEGS, out_dtypes)],
        compiler_params=pltpu.CompilerParams(dimension_semantics=("parallel",)),
        name="in_proj",
    )(x2d, g, w_perm)


def _kv_t_kernel(x_ref, g_ref, wt_ref, o_ref):
    xn = _rms(x_ref[...], g_ref[...]).astype(BF16)
    o_ref[...] = lax.dot_general(wt_ref[...], xn, _NT, preferred_element_type=F32)


def _kv_t(x2d, g, wt):
    m = x2d.shape[0]
    n = wt.shape[0]
    return pl.pallas_call(
        _kv_t_kernel,
        grid=(1,),
        in_specs=[_const_spec((m, D_MODEL)), _const_spec((1, D_MODEL)), _const_spec((n, D_MODEL))],
        out_specs=pl.BlockSpec((n, m), lambda i: (0, 0)),
        out_shape=jax.ShapeDtypeStruct((n, m), F32),
        name="kv_t",
    )(x2d, g, wt)


def _attn_prompt_kernel(q_ref, kp_ref, kc_ref, vp_ref, vc_ref, o_ref, l_ref):
    i = pl.program_id(2)
    qi = lax.broadcasted_iota(jnp.int32, (BAND, 2 * BAND), 0)
    kj = lax.broadcasted_iota(jnp.int32, (BAND, 2 * BAND), 1)
    dist = BAND + qi - kj
    valid = (dist >= 0) & (dist <= BAND) & ((kj >= BAND) | (i > 0))
    q = q_ref[0]
    k = jnp.concatenate([kp_ref[0], kc_ref[0]], axis=0).astype(BF16)
    v = jnp.concatenate([vp_ref[0], vc_ref[0]], axis=0).astype(BF16)
    outs, lses = [], []
    for h in range(HEADS_PER_BRANCH):
        sl = slice(h * HEAD_DIM, (h + 1) * HEAD_DIM)
        s = lax.dot_general(q[:, sl], k[:, sl], _NT, preferred_element_type=F32)
        s = jnp.where(valid, s, NEG_INF)
        m = jnp.max(s, axis=-1, keepdims=True)
        p = jnp.exp(s - m)
        den = jnp.sum(p, axis=-1, keepdims=True)
        o = jnp.dot(p.astype(BF16), v[:, sl], preferred_element_type=F32) / den
        outs.append(o)
        lses.append(jnp.broadcast_to(m + jnp.log(den), (BAND, HEAD_DIM)))
    o_ref[0] = jnp.concatenate(outs, axis=1)
    l_ref[0] = jnp.concatenate(lses, axis=1)


def _attn_prompt(q, kv, g, dil):
    b, t, _ = q.shape
    tc = t // dil
    nblk = tc // BAND
    qv = q.reshape(b, tc, dil * D_ATT)
    kvv = kv.reshape(b, tc, dil * KV_W)
    blk = (1, BAND, D_BRANCH)
    prev = lambda i: jnp.maximum(i - 1, 0)
    o, l = pl.pallas_call(
        _attn_prompt_kernel,
        grid=(b, dil, nblk),
        in_specs=[
            pl.BlockSpec(blk, lambda bi, r, i: (bi, i, r * N_BRANCH + g)),
            pl.BlockSpec(blk, lambda bi, r, i: (bi, prev(i), 2 * r)),
            pl.BlockSpec(blk, lambda bi, r, i: (bi, i, 2 * r)),
            pl.BlockSpec(blk, lambda bi, r, i: (bi, prev(i), 2 * r + 1)),
            pl.BlockSpec(blk, lambda bi, r, i: (bi, i, 2 * r + 1)),
        ],
        out_specs=[pl.BlockSpec(blk, lambda bi, r, i: (bi, i, r))] * 2,
        out_shape=[jax.ShapeDtypeStruct((b, tc, dil * D_BRANCH), F32)] * 2,
        compiler_params=pltpu.CompilerParams(dimension_semantics=("parallel", "parallel", "arbitrary")),
        name="attn_prompt_d%d" % dil,
    )(qv, kvv, kvv, kvv, kvv)
    return o.reshape(b, t, D_BRANCH), l.reshape(b, t, D_BRANCH)


_QROWS = 4 * SUBLANES
_SHIFT_ROWS = 64


def _sample_attn_kernel(n_new, q_ref, nt_ref, c1_ref, c4_ref, c16_ref,
                        att_ref, o1_ref, o4_ref, o16_ref, nbuf):
    b = pl.program_id(0)
    per_tile = LANES // n_new
    new0 = LANES - n_new
    shift = new0 - n_new * lax.rem(b, per_tile)
    nbuf[...] = pltpu.roll(nt_ref[...], shift, axis=1)

    q = q_ref[0]
    row = lax.broadcasted_iota(jnp.int32, (_QROWS, D_BRANCH), 0)
    lane = lax.broadcasted_iota(jnp.int32, (_QROWS, D_BRANCH), 1)
    hmask = (lane // HEAD_DIM) == (row % SUBLANES)

    us, dens, lses = [], [], []
    for g, ((win, dil), c_ref) in enumerate(zip(ATT_BRANCHES, (c1_ref, c4_ref, c16_ref))):
        length = c_ref.shape[2]
        qs = q[:, g * D_BRANCH:(g + 1) * D_BRANCH]
        qb = jnp.concatenate([jnp.broadcast_to(qs[s:s + 1], (SUBLANES, D_BRANCH)) for s in range(n_new)], axis=0)
        qbd = jnp.where(hmask, qb, 0.0).astype(BF16)
        kt = c_ref[0, 0:D_BRANCH, :].astype(BF16)
        vt = c_ref[0, D_BRANCH:KV_W, :].astype(BF16)
        kn = nbuf[g * KV_W:g * KV_W + D_BRANCH, :].astype(BF16)
        vn = nbuf[g * KV_W + D_BRANCH:(g + 1) * KV_W, :].astype(BF16)
        sc = jnp.dot(qbd, kt, preferred_element_type=F32)
        scn = jnp.dot(qbd, kn, preferred_element_type=F32)
        s_c = lax.broadcasted_iota(jnp.int32, (_QROWS, length), 0) // SUBLANES
        l_c = lax.broadcasted_iota(jnp.int32, (_QROWS, length), 1)
        valid_c = (l_c >= s_c) & (((l_c - s_c) & (dil - 1)) == 0)
        s_n = lax.broadcasted_iota(jnp.int32, (_QROWS, LANES), 0) // SUBLANES
        j_n = lax.broadcasted_iota(jnp.int32, (_QROWS, LANES), 1) - new0
        valid_n = (j_n >= 0) & (j_n <= s_n) & (((s_n - j_n) & (dil - 1)) == 0)
        sc = jnp.where(valid_c, sc, NEG_INF)
        scn = jnp.where(valid_n, scn, NEG_INF)
        m = jnp.maximum(jnp.max(sc, axis=-1, keepdims=True), jnp.max(scn, axis=-1, keepdims=True))
        p = jnp.exp(sc - m)
        pn = jnp.exp(scn - m)
        den = jnp.sum(p, axis=-1, keepdims=True) + jnp.sum(pn, axis=-1, keepdims=True)
        u = (lax.dot_general(p.astype(BF16), vt, _NT, preferred_element_type=F32)
             + lax.dot_general(pn.astype(BF16), vn, _NT, preferred_element_type=F32))
        us.append(u)
        dens.append(den)
        lses.append(m + jnp.log(den))

    mx = jnp.maximum(jnp.maximum(lses[0], lses[1]), lses[2])
    es = [jnp.exp(l - mx) for l in lses]
    tot = es[0] + es[1] + es[2]
    for g in range(N_BRANCH):
        coef = es[g] / tot / dens[g]
        a = jnp.where(hmask, us[g] * coef, 0.0)
        a = a.reshape(n_new, SUBLANES, D_BRANCH).sum(axis=1)
        att_ref[0, :, g * D_BRANCH:(g + 1) * D_BRANCH] = a

    lane_t = lax.broadcasted_iota(jnp.int32, (_SHIFT_ROWS, LANES), 1)
    keep = lane_t < new0
    for g, (c_ref, o_ref) in enumerate(zip((c1_ref, c4_ref, c16_ref), (o1_ref, o4_ref, o16_ref))):
        ntiles = c_ref.shape[2] // LANES

        def body(i, carry, g=g, c_ref=c_ref, o_ref=o_ref, ntiles=ntiles):
            r0 = pl.multiple_of(i * _SHIFT_ROWS, _SHIFT_ROWS)
            nxt = pltpu.roll(c_ref[0, pl.ds(r0, _SHIFT_ROWS), 0:LANES], new0, axis=1)
            for j in range(ntiles):
                cur = nxt
                if j + 1 < ntiles:
                    nxt = pltpu.roll(c_ref[0, pl.ds(r0, _SHIFT_ROWS), (j + 1) * LANES:(j + 2) * LANES], new0, axis=1)
                else:
                    nxt = nbuf[pl.ds(g * KV_W + r0, _SHIFT_ROWS), :]
                o_ref[0, pl.ds(r0, _SHIFT_ROWS), j * LANES:(j + 1) * LANES] = jnp.where(keep, cur, nxt)
            return carry

        lax.fori_loop(0, KV_W // _SHIFT_ROWS, body, 0)


def _sample_attn(q_s, new_t, caches_t):
    b, n_new, _ = q_s.shape
    per_tile = LANES // n_new
    cache_specs = [pl.BlockSpec((1, KV_W, c.shape[2]), lambda i: (i, 0, 0)) for c in caches_t]
    outs = pl.pallas_call(
        functools.partial(_sample_attn_kernel, n_new),
        grid=(b,),
        in_specs=[
            pl.BlockSpec((1, n_new, D_ATT), lambda i: (i, 0, 0)),
            pl.BlockSpec((N_BRANCH * KV_W, LANES), lambda i: (0, i // per_tile)),
        ] + cache_specs,
        out_specs=[pl.BlockSpec((1, n_new, D_ATT), lambda i: (i, 0, 0))] + cache_specs,
        out_shape=[jax.ShapeDtypeStruct((b, n_new, D_ATT), F32)]
        + [jax.ShapeDtypeStruct(c.shape, F32) for c in caches_t],
        scratch_shapes=[pltpu.VMEM((N_BRANCH * KV_W, LANES), F32)],
        compiler_params=pltpu.CompilerParams(dimension_semantics=("parallel",)),
        name="sample_attn_shift",
    )(q_s, new_t, *caches_t)
    return outs[0], outs[1:]


_XPAD_ROWS = SSD_CHUNK + SUBLANES
_GROUP_W = SSD_HEADS_PER_GROUP * SSD_HEAD_DIM
_STATE_W = 2 * LANES


def _ssd_kernel(valid_rows, xbc_ref, z_ref, dt_ref, conv0_ref, ssm0_ref, cw_ref, cb_ref,
                dtb_ref, alog_ref, dsk_ref, nrm_ref, y_ref, ssm_ref, xpad, st, ybuf):
    c = pl.program_id(1)
    nc = pl.num_programs(1)
    cl = SSD_CHUNK

    @pl.when(c == 0)
    def _():
        xpad[0:SUBLANES, :] = conv0_ref[0]
        zpad = jnp.zeros((_STATE_W - _GROUP_W, SSD_STATE), F32)
        for g in range(SSD_GROUPS):
            s = jnp.concatenate([ssm0_ref[0, SSD_HEADS_PER_GROUP * g + r] for r in range(SSD_HEADS_PER_GROUP)]
                                + [zpad], axis=0)
            st[g] = s.T

    @pl.when(c > 0)
    def _():
        xpad[0:SUBLANES, :] = xpad[cl:cl + SUBLANES, :]

    xpad[SUBLANES:SUBLANES + cl, :] = xbc_ref[0]
    xc = cb_ref[...]
    for k in range(CONV_WIDTH):
        off = SUBLANES - (CONV_WIDTH - 1) + k
        xc = xc + cw_ref[k:k + 1, :] * xpad[off:off + cl, :]
    xc = xc * _sigmoid(xc)
    xs = xc[:, :D_INNER]
    bm = xc[:, D_INNER:D_INNER + SSD_GROUPS * SSD_STATE]
    cm = xc[:, D_INNER + SSD_GROUPS * SSD_STATE:]

    r_i = lax.broadcasted_iota(jnp.int32, (cl, cl), 0)
    c_i = lax.broadcasted_iota(jnp.int32, (cl, cl), 1)
    causal = r_i >= c_i
    dt = _softplus(dt_ref[0] + dtb_ref[...])
    if valid_rows < cl:
        dt = jnp.where(r_i < valid_rows, dt, 0.0)
    da = dt * (-jnp.exp(alog_ref[...]))
    hi = lax.Precision.HIGHEST
    acum = jnp.dot(causal.astype(F32), da, precision=hi, preferred_element_type=F32)
    e_r = lax.broadcasted_iota(jnp.int32, (LANES, D_INNER), 0)
    e_c = lax.broadcasted_iota(jnp.int32, (LANES, D_INNER), 1)
    expand = (e_c // SSD_HEAD_DIM == e_r).astype(F32)
    dt_e = jnp.dot(dt, expand, precision=hi, preferred_element_type=F32)
    acum_e = jnp.dot(acum, expand, precision=hi, preferred_element_type=F32)
    tot_e = acum_e[cl - 1:cl, :]
    xw = xs * (jnp.exp(tot_e - acum_e) * dt_e)
    eac_e = jnp.exp(acum_e)
    cd_e = jnp.exp(tot_e)
    acum_t = acum.T
    dt_t = dt.T

    yoffs = []
    for g in range(SSD_GROUPS):
        bg = bm[:, g * SSD_STATE:(g + 1) * SSD_STATE]
        cgb = cm[:, g * SSD_STATE:(g + 1) * SSD_STATE].astype(BF16)
        cb = lax.dot_general(cgb, bg.astype(BF16), _NT, preferred_element_type=F32)
        sg = st[g]
        yoffs.append(jnp.dot(cgb, sg.astype(BF16), preferred_element_type=F32)[:, :_GROUP_W])
        gsl = slice(g * _GROUP_W, (g + 1) * _GROUP_W)
        snew = jnp.dot(bg.T.astype(BF16), xw[:, gsl].astype(BF16), preferred_element_type=F32)
        st[g, :, 0:_GROUP_W] = cd_e[:, gsl] * sg[:, :_GROUP_W] + snew
        for r in range(SSD_HEADS_PER_GROUP):
            h = SSD_HEADS_PER_GROUP * g + r
            seg = acum[:, h:h + 1] - acum_t[h:h + 1, :]
            lmat = jnp.exp(jnp.where(causal, seg, NEG_INF))
            mm = cb * lmat * dt_t[h:h + 1, :]
            hsl = slice(h * SSD_HEAD_DIM, (h + 1) * SSD_HEAD_DIM)
            ybuf[:, hsl] = jnp.dot(mm.astype(BF16), xs[:, hsl].astype(BF16), preferred_element_type=F32)

    y = ybuf[...] + jnp.concatenate(yoffs, axis=1) * eac_e + dsk_ref[...] * xs
    z = z_ref[0]
    y = y * (z * _sigmoid(z))
    y_ref[0] = _rms(y, nrm_ref[...])

    @pl.when(c == nc - 1)
    def _():
        for g in range(SSD_GROUPS):
            t = st[g].T
            for r in range(SSD_HEADS_PER_GROUP):
                ssm_ref[0, SSD_HEADS_PER_GROUP * g + r] = t[r * SSD_HEAD_DIM:(r + 1) * SSD_HEAD_DIM, :]


def _ssd(xbc, z, dt, conv0, ssm0, params, valid_rows):
    b, t, _ = xbc.shape
    cl = SSD_CHUNK
    cw, cb, dtb, alog, dsk, nrm = params
    tok = lambda w: pl.BlockSpec((1, cl, w), lambda bi, ci: (bi, ci, 0))
    ssm_spec = pl.BlockSpec((1, SSD_HEADS, SSD_HEAD_DIM, SSD_STATE), lambda bi, ci: (bi, 0, 0, 0))
    return pl.pallas_call(
        functools.partial(_ssd_kernel, valid_rows),
        grid=(b, t // cl),
        in_specs=[
            tok(CONV_DIM), tok(D_INNER), tok(DT_PAD),
            pl.BlockSpec((1, SUBLANES, CONV_DIM), lambda bi, ci: (bi, 0, 0)),
            ssm_spec,
            _const_spec((CONV_WIDTH, CONV_DIM)), _const_spec((1, CONV_DIM)),
            _const_spec((1, DT_PAD)), _const_spec((1, DT_PAD)),
            _const_spec((1, D_INNER)), _const_spec((1, D_INNER)),
        ],
        out_specs=[tok(D_INNER), ssm_spec],
        out_shape=[jax.ShapeDtypeStruct((b, t, D_INNER), F32),
                   jax.ShapeDtypeStruct((b, SSD_HEADS, SSD_HEAD_DIM, SSD_STATE), F32)],
        scratch_shapes=[pltpu.VMEM((_XPAD_ROWS, CONV_DIM), F32),
                        pltpu.VMEM((SSD_GROUPS, SSD_STATE, _STATE_W), F32),
                        pltpu.VMEM((cl, D_INNER), F32)],
        compiler_params=pltpu.CompilerParams(dimension_semantics=("parallel", "arbitrary")),
        name="ssd",
    )(xbc, z, dt, conv0, ssm0, cw, cb, dtb, alog, dsk, nrm)


def _out_ffn_kernel(merged, final, x_ref, *refs):
    n_att = 1 if merged else 2 * N_BRANCH
    att_refs = refs[:n_att]
    ssd_ref, wo_ref, gffn_ref, wg_ref, wu_ref, wd_ref, gfin_ref, y_ref, act_ref = refs[n_att:]
    h = x_ref[...]
    if merged:
        h = h + jnp.dot(att_refs[0][...].astype(BF16), wo_ref[0:D_ATT, :], preferred_element_type=F32)
    else:
        ls = [r[...] for r in att_refs[N_BRANCH:]]
        mx = jnp.maximum(jnp.maximum(ls[0], ls[1]), ls[2])
        es = [jnp.exp(l - mx) for l in ls]
        inv = 1.0 / (es[0] + es[1] + es[2])
        for g in range(N_BRANCH):
            a = (att_refs[g][...] * (es[g] * inv)).astype(BF16)
            h = h + jnp.dot(a, wo_ref[g * D_BRANCH:(g + 1) * D_BRANCH, :], preferred_element_type=F32)
    h = h + jnp.dot(ssd_ref[...].astype(BF16), wo_ref[D_ATT:D_MIX, :], preferred_element_type=F32)
    hn = _rms(h, gffn_ref[...]).astype(BF16)
    for c in range(0, D_FF, FF_CHUNK):
        gate = jnp.dot(hn, wg_ref[:, c:c + FF_CHUNK], preferred_element_type=F32)
        up = jnp.dot(hn, wu_ref[:, c:c + FF_CHUNK], preferred_element_type=F32)
        act_ref[:, c:c + FF_CHUNK] = (gate * _sigmoid(gate) * up).astype(BF16)
    y = h + jnp.dot(act_ref[...], wd_ref[...], preferred_element_type=F32)
    y_ref[...] = _rms(y, gfin_ref[...]) if final else y


def _out_ffn(x2d, att_parts, ssd2d, weights, tm, merged, final):
    m = x2d.shape[0]
    wo, gffn, wg, wu, wd, gfin = weights
    tok = lambda w: pl.BlockSpec((tm, w), lambda i: (i, 0))
    return pl.pallas_call(
        functools.partial(_out_ffn_kernel, merged, final),
        grid=(m // tm,),
        in_specs=[tok(D_MODEL)] + [tok(a.shape[1]) for a in att_parts] + [
            tok(D_INNER),
            _const_spec((D_MIX, D_MODEL)), _const_spec((1, D_MODEL)),
            _const_spec((D_MODEL, D_FF)), _const_spec((D_MODEL, D_FF)),
            _const_spec((D_FF, D_MODEL)), _const_spec((1, D_MODEL)),
        ],
        out_specs=tok(D_MODEL),
        out_shape=jax.ShapeDtypeStruct((m, D_MODEL), F32),
        scratch_shapes=[pltpu.VMEM((tm, D_FF), BF16)],
        compiler_params=pltpu.CompilerParams(dimension_semantics=("parallel",)),
        name="out_ffn_merged" if merged else "out_ffn",
    )(x2d, *att_parts, ssd2d, wo, gffn, wg, wu, wd, gfin)


def _pad_lanes(v, width):
    return jnp.pad(v, ((0, 0), (0, width - v.shape[1])))


def kernel(x_prompt, x_sample, cache_kv_d1, cache_kv_d4, cache_kv_d16, state_conv, state_ssm, norm_mix, w_in, conv_w, conv_b, dt_bias, a_log, d_skip, norm_ssd, w_out, norm_ffn, w_gate_up, w_down, norm_final):
    bp, tp, _ = x_prompt.shape
    bs, ts, _ = x_sample.shape
    caches = (cache_kv_d1, cache_kv_d4, cache_kv_d16)
    hp, hs = x_prompt.reshape(bp * tp, D_MODEL), x_sample.reshape(bs * ts, D_MODEL)
    gfin = norm_final.reshape(1, D_MODEL)
    p_new, s_new = [[] for _ in range(5)], [[] for _ in range(5)]

    for l in range(w_in.shape[0]):
        wi = w_in[l]
        wq, wk, wv = (wi[:, i * D_ATT:(i + 1) * D_ATT] for i in range(3))
        wkv = [jnp.concatenate([wk[:, g * D_BRANCH:(g + 1) * D_BRANCH], wv[:, g * D_BRANCH:(g + 1) * D_BRANCH]], axis=1)
               for g in range(N_BRANCH)]
        o = 3 * D_ATT
        w_rest = wi[:, o:o + D_INNER + CONV_DIM]
        w_dt = _pad_lanes(wi[:, o + D_INNER + CONV_DIM:], DT_PAD)
        w_perm = jnp.concatenate([wq] + wkv + [w_rest, w_dt], axis=1).astype(BF16)
        w_kv_t = jnp.concatenate(wkv, axis=1).T.astype(BF16)
        gmix = norm_mix[l].reshape(1, D_MODEL)
        ssd_params = (conv_w[l], conv_b[l].reshape(1, CONV_DIM),
                      _pad_lanes(dt_bias[l].reshape(1, SSD_HEADS), DT_PAD),
                      _pad_lanes(a_log[l].reshape(1, SSD_HEADS), DT_PAD),
                      jnp.repeat(d_skip[l], SSD_HEAD_DIM).reshape(1, D_INNER),
                      norm_ssd[l].reshape(1, D_INNER))
        ffn_w = (w_out[l].astype(BF16), norm_ffn[l].reshape(1, D_MODEL),
                 w_gate_up[l][:, :D_FF].astype(BF16), w_gate_up[l][:, D_FF:].astype(BF16),
                 w_down[l].astype(BF16), gfin)
        final = l == w_in.shape[0] - 1

        q, kv1, kv4, kv16, z, xbc, dt = _in_proj(hp, gmix, w_perm, 256)
        kvs = (kv1, kv4, kv16)
        q3 = q.reshape(bp, tp, D_ATT)
        att_o, att_l = [], []
        for g, (win, dil) in enumerate(ATT_BRANCHES):
            og, lg = _attn_prompt(q3, kvs[g].reshape(bp, tp, KV_W), g, dil)
            att_o.append(og.reshape(bp * tp, D_BRANCH))
            att_l.append(lg.reshape(bp * tp, D_BRANCH))
            keep = min(win, tp)
            rows = kvs[g].reshape(bp, tp, 2, HEADS_PER_BRANCH, HEAD_DIM)[:, tp - keep:]
            p_new[g].append(rows)
        xbc3 = xbc.reshape(bp, tp, CONV_DIM)
        conv0 = jnp.zeros((bp, SUBLANES, CONV_DIM), F32)
        ssm0 = jnp.zeros((bp, SSD_HEADS, SSD_HEAD_DIM, SSD_STATE), F32)
        y_ssd, ssm_p = _ssd(xbc3, z.reshape(bp, tp, D_INNER), dt.reshape(bp, tp, DT_PAD), conv0, ssm0,
                            ssd_params, SSD_CHUNK)
        p_new[3].append(xbc3[:, tp - (CONV_WIDTH - 1):])
        p_new[4].append(ssm_p)
        hp = _out_ffn(hp, att_o + att_l, y_ssd.reshape(bp * tp, D_INNER), ffn_w, 512, merged=False, final=final)

        q, _, _, _, z, xbc, dt = _in_proj(hs, gmix, w_perm, 256)
        new_t = _kv_t(hs, gmix, w_kv_t)
        caches_t = [jnp.transpose(c[l], (0, 2, 3, 4, 1)).reshape(bs, KV_W, c.shape[2]) for c in caches]
        att, shifted = _sample_attn(q.astype(F32).reshape(bs, ts, D_ATT), new_t, caches_t)
        for g in range(N_BRANCH):
            lg = shifted[g].shape[2]
            s_new[g].append(jnp.transpose(shifted[g].reshape(bs, 2, HEADS_PER_BRANCH, HEAD_DIM, lg), (0, 4, 1, 2, 3)))
        pad_t = lambda a: jnp.pad(a.reshape(bs, ts, a.shape[1]), ((0, 0), (0, SSD_CHUNK - ts), (0, 0)))
        xbc3 = xbc.reshape(bs, ts, CONV_DIM)
        conv0 = jnp.pad(state_conv[l], ((0, 0), (SUBLANES - (CONV_WIDTH - 1), 0), (0, 0)))
        y_ssd, ssm_s = _ssd(pad_t(xbc), pad_t(z), pad_t(dt), conv0, state_ssm[l], ssd_params, ts)
        s_new[3].append(jnp.concatenate([state_conv[l], xbc3], axis=1)[:, -(CONV_WIDTH - 1):])
        s_new[4].append(ssm_s)
        hs = _out_ffn(hs, [att.reshape(bs * ts, D_ATT)], y_ssd[:, :ts].reshape(bs * ts, D_INNER), ffn_w,
                      256, merged=True, final=final)

    y_prompt = hp.reshape(bp, tp, D_MODEL)
    y_sample = hs.reshape(bs, ts, D_MODEL)
    p_out = [jnp.stack(a) for a in p_new]
    s_out = [jnp.stack(a) for a in s_new]
    return (y_prompt, y_sample, *p_out, *s_out)
```

```python
import functools

import jax
import jax.numpy as jnp
from jax import lax
from jax.experimental import pallas as pl
from jax.experimental.pallas import tpu as pltpu

F32 = jnp.float32
BF16 = jnp.bfloat16

D_MODEL = 1024
HEAD_DIM = 64
ATT_BRANCHES = ((128, 1), (512, 4), (2048, 16))
N_BRANCH = 3
HEADS_PER_BRANCH = 4
D_BRANCH = HEADS_PER_BRANCH * HEAD_DIM
D_ATT = N_BRANCH * D_BRANCH
SSD_HEADS = 12
SSD_HEAD_DIM = 64
D_INNER = SSD_HEADS * SSD_HEAD_DIM
SSD_GROUPS = 4
SSD_HEADS_PER_GROUP = 3
SSD_STATE = 128
CONV_WIDTH = 4
CONV_DIM = D_INNER + 2 * SSD_GROUPS * SSD_STATE
SSD_CHUNK = 128
D_MIX = D_ATT + D_INNER
D_FF = 2816
RMS_EPS = 1e-5
NEG_INF = -1e30
ATT_SCALE = HEAD_DIM ** -0.5
BAND = 128

LANES = 128
SUBLANES = 8
KV_W = 2 * D_BRANCH
DT_PAD = LANES
FF_CHUNK = 256
PROJ_CHUNK = 256
PROJ_TM = 256
ATT_QB = 2 * BAND
FFN_TM = 512

_NT = (((1,), (1,)), ((), ()))


def _rms(x, g):
    inv = lax.rsqrt(jnp.mean(x * x, axis=-1, keepdims=True) + RMS_EPS)
    return x * inv * g


def _sigmoid(x):
    return 1.0 / (1.0 + jnp.exp(-x))


def _softplus(x):
    return jnp.maximum(x, 0.0) + jnp.log1p(jnp.exp(-jnp.abs(x)))


def _const_spec(shape):
    nd = len(shape)
    return pl.BlockSpec(shape, lambda *_: (0,) * nd, pipeline_mode=pl.Buffered(1))


def _proj_store(xn, w_ref, out_ref, width):
    for c in range(0, width, PROJ_CHUNK):
        w = min(PROJ_CHUNK, width - c)
        out_ref[:, c:c + w] = jnp.dot(xn, w_ref[:, c:c + w], preferred_element_type=F32)


def _store_classes(scr, acc, out_ref, dil):
    if dil == 1:
        out_ref[0] = acc.astype(BF16)
        return
    n = acc.shape[0] // dil
    for half in range(D_BRANCH // LANES):
        scr[half] = acc[:, half * LANES:(half + 1) * LANES]
    for r in range(dil):
        for half in range(D_BRANCH // LANES):
            c0 = r * D_BRANCH + half * LANES
            out_ref[0, :, c0:c0 + LANES] = scr[half, pl.ds(r, n, stride=dil), :].astype(BF16)


def _in_proj_prompt_kernel(tiles_per_batch, x_ref, g_ref, wq_ref, wkv_ref, wzx_ref, wdt_ref, *refs):
    qkv_refs = refs[:3 * N_BRANCH]
    kt_refs = refs[3 * N_BRANCH:4 * N_BRANCH]
    z_ref, xbc_ref, dt_ref, scr = refs[4 * N_BRANCH:]
    tm = x_ref.shape[0]
    ti = lax.rem(pl.program_id(0), tiles_per_batch)
    xn = _rms(x_ref[...], g_ref[...]).astype(BF16)
    for g, (win, dil) in enumerate(ATT_BRANCHES):
        q_ref, k_ref, v_ref = qkv_refs[3 * g:3 * g + 3]
        q = jnp.dot(xn, wq_ref[:, g * D_BRANCH:(g + 1) * D_BRANCH], preferred_element_type=F32) * ATT_SCALE
        _store_classes(scr, q, q_ref, dil)
        kv = jnp.dot(xn, wkv_ref[:, g * KV_W:(g + 1) * KV_W], preferred_element_type=F32)
        _store_classes(scr, kv[:, :D_BRANCH], k_ref, dil)
        _store_classes(scr, kv[:, D_BRANCH:], v_ref, dil)
        w = kt_refs[g].shape[2]
        first = tiles_per_batch - win // w

        @pl.when(ti >= first)
        def _(kv=kv, kt_ref=kt_refs[g], w=w):
            kt_ref[0] = kv[tm - w:, :].T

    _proj_store(xn, wzx_ref.at[:, 0:D_INNER], z_ref, D_INNER)
    _proj_store(xn, wzx_ref.at[:, D_INNER:D_INNER + CONV_DIM], xbc_ref, CONV_DIM)
    _proj_store(xn, wdt_ref, dt_ref, DT_PAD)


def _in_proj_prompt(x2d, b, t, g, wq, wkv, wzx, wdt):
    m = b * t
    tm = PROJ_TM
    tpb = t // tm
    tok = lambda w: pl.BlockSpec((tm, w), lambda i: (i, 0))
    qkv_specs, qkv_shapes, kt_specs, kt_shapes = [], [], [], []
    for win, dil in ATT_BRANCHES:
        spec = pl.BlockSpec((1, tm // dil, dil * D_BRANCH), lambda i: (i // tpb, i % tpb, 0))
        shape = jax.ShapeDtypeStruct((b, t // dil, dil * D_BRANCH), BF16)
        qkv_specs += [spec] * 3
        qkv_shapes += [shape] * 3
        w = min(win, tm)
        first = tpb - win // w
        kt_specs.append(pl.BlockSpec((1, KV_W, w), lambda i, first=first: (i // tpb, 0, jnp.maximum(i % tpb - first, 0))))
        kt_shapes.append(jax.ShapeDtypeStruct((b, KV_W, win), F32))
    outs = pl.pallas_call(
        functools.partial(_in_proj_prompt_kernel, tpb),
        grid=(m // tm,),
        in_specs=[tok(D_MODEL), _const_spec((1, D_MODEL)), _const_spec((D_MODEL, D_ATT)),
                  _const_spec((D_MODEL, N_BRANCH * KV_W)), _const_spec((D_MODEL, D_INNER + CONV_DIM)),
                  _const_spec((D_MODEL, DT_PAD))],
        out_specs=qkv_specs + kt_specs + [tok(D_INNER), tok(CONV_DIM), tok(DT_PAD)],
        out_shape=qkv_shapes + kt_shapes + [jax.ShapeDtypeStruct((m, w), F32) for w in (D_INNER, CONV_DIM, DT_PAD)],
        scratch_shapes=[pltpu.VMEM((D_BRANCH // LANES, tm, LANES), F32)],
        compiler_params=pltpu.CompilerParams(dimension_semantics=("arbitrary",)),
        name="in_proj_prompt",
    )(x2d, g, wq, wkv, wzx, wdt)
    n = 3 * N_BRANCH
    return outs[:n], outs[n:n + N_BRANCH], outs[n + N_BRANCH:]


def _in_proj_sample_kernel(x_ref, g_ref, wq_ref, wzx_ref, wdt_ref, wkvt_ref, q_ref, z_ref, xbc_ref, dt_ref, kvt_ref):
    xn = _rms(x_ref[...], g_ref[...]).astype(BF16)
    for c in range(0, D_ATT, PROJ_CHUNK):
        q = jnp.dot(xn, wq_ref[:, c:c + PROJ_CHUNK], preferred_element_type=F32) * ATT_SCALE
        q_ref[:, c:c + PROJ_CHUNK] = q.astype(BF16).astype(F32)
    _proj_store(xn, wzx_ref.at[:, 0:D_INNER], z_ref, D_INNER)
    _proj_store(xn, wzx_ref.at[:, D_INNER:D_INNER + CONV_DIM], xbc_ref, CONV_DIM)
    _proj_store(xn, wdt_ref, dt_ref, DT_PAD)
    kvt_ref[...] = lax.dot_general(wkvt_ref[...], xn, _NT, preferred_element_type=F32)


def _in_proj_sample(x2d, g, wq, wzx, wdt, wkvt):
    m = x2d.shape[0]
    widths = (D_ATT, D_INNER, CONV_DIM, DT_PAD)
    ins = (x2d, g, wq, wzx, wdt, wkvt)
    return pl.pallas_call(
        _in_proj_sample_kernel,
        grid=(1,),
        in_specs=[_const_spec(a.shape) for a in ins],
        out_specs=[pl.BlockSpec((m, w), lambda i: (0, 0)) for w in widths]
        + [pl.BlockSpec((N_BRANCH * KV_W, m), lambda i: (0, 0))],
        out_shape=[jax.ShapeDtypeStruct((m, w), F32) for w in widths]
        + [jax.ShapeDtypeStruct((N_BRANCH * KV_W, m), F32)],
        name="in_proj_sample",
    )(*ins)


_ATT_STEPS = 16


def _attn_block(q, k, v, has_prev):
    qi = lax.broadcasted_iota(jnp.int32, (BAND, 2 * BAND), 0)
    kj = lax.broadcasted_iota(jnp.int32, (BAND, 2 * BAND), 1)
    dist = BAND + qi - kj
    valid = (dist >= 0) & (dist <= BAND) & ((kj >= BAND) | has_prev)
    outs, lses = [], []
    for h in range(HEADS_PER_BRANCH):
        sl = slice(h * HEAD_DIM, (h + 1) * HEAD_DIM)
        s = lax.dot_general(q[:, sl], k[:, sl], _NT, preferred_element_type=F32)
        s = jnp.where(valid, s, NEG_INF)
        m = jnp.max(s, axis=-1, keepdims=True)
        p = jnp.exp(s - m)
        den = jnp.sum(p, axis=-1, keepdims=True)
        o = jnp.dot(p.astype(BF16), v[:, sl], preferred_element_type=F32) / den
        outs.append(o)
        lses.append(jnp.broadcast_to(m + jnp.log(den), (BAND, HEAD_DIM)))
    return jnp.concatenate(outs, axis=1), jnp.concatenate(lses, axis=1)


def _attn_prompt_kernel(*refs):
    ins, outs = refs[:5 * N_BRANCH], refs[5 * N_BRANCH:]
    j = pl.program_id(1)
    for g, (win, dil) in enumerate(ATT_BRANCHES):
        q_ref, kp_ref, kc_ref, vp_ref, vc_ref = ins[5 * g:5 * g + 5]
        o_ref, l_ref = outs[2 * g:2 * g + 2]
        i = lax.rem(j, _ATT_STEPS // dil)
        kc, vc = kc_ref[0], vc_ref[0]
        for sb in range(ATT_QB // BAND):
            rows = slice(sb * BAND, (sb + 1) * BAND)
            if sb == 0:
                k = jnp.concatenate([kp_ref[0], kc[rows]], axis=0)
                v = jnp.concatenate([vp_ref[0], vc[rows]], axis=0)
                has_prev = i > 0
            else:
                k = kc[(sb - 1) * BAND:(sb + 1) * BAND]
                v = vc[(sb - 1) * BAND:(sb + 1) * BAND]
                has_prev = True
            o, l = _attn_block(q_ref[0, rows, :], k, v, has_prev)
            o_ref[0, rows, :] = o
            l_ref[0, rows, :] = l


def _attn_prompt(qkv, b, t):
    assert t // ATT_QB == _ATT_STEPS
    in_specs, args, out_specs, out_shapes = [], [], [], []
    for g, (win, dil) in enumerate(ATT_BRANCHES):
        nblk = _ATT_STEPS // dil
        cur = pl.BlockSpec((1, ATT_QB, D_BRANCH), lambda bi, j, nblk=nblk: (bi, j % nblk, j // nblk))
        prv = pl.BlockSpec((1, BAND, D_BRANCH),
                           lambda bi, j, nblk=nblk: (bi, jnp.maximum((ATT_QB // BAND) * (j % nblk) - 1, 0), j // nblk))
        q, k, v = qkv[3 * g:3 * g + 3]
        in_specs += [cur, prv, cur, prv, cur]
        args += [q, k, k, v, v]
        out_specs += [cur, cur]
        out_shapes += [jax.ShapeDtypeStruct((b, t // dil, dil * D_BRANCH), F32)] * 2
    return pl.pallas_call(
        _attn_prompt_kernel,
        grid=(b, _ATT_STEPS),
        in_specs=in_specs,
        out_specs=out_specs,
        out_shape=out_shapes,
        compiler_params=pltpu.CompilerParams(dimension_semantics=("parallel", "arbitrary")),
        name="attn_prompt",
    )(*args)


_QROWS = 4 * SUBLANES
_SHIFT_ROWS = 64


def _sample_attn_kernel(n_new, q_ref, nt_ref, c1_ref, c4_ref, c16_ref,
                        att_ref, o1_ref, o4_ref, o16_ref, nbuf):
    b = pl.program_id(0)
    per_tile = LANES // n_new
    new0 = LANES - n_new
    shift = new0 - n_new * lax.rem(b, per_tile)
    nbuf[...] = pltpu.roll(nt_ref[...], shift, axis=1)

    q = q_ref[0]
    row = lax.broadcasted_iota(jnp.int32, (_QROWS, D_BRANCH), 0)
    lane = lax.broadcasted_iota(jnp.int32, (_QROWS, D_BRANCH), 1)
    hmask = (lane // HEAD_DIM) == (row % SUBLANES)

    us, dens, lses = [], [], []
    for g, ((win, dil), c_ref) in enumerate(zip(ATT_BRANCHES, (c1_ref, c4_ref, c16_ref))):
        length = c_ref.shape[2]
        qs = q[:, g * D_BRANCH:(g + 1) * D_BRANCH]
        qb = jnp.concatenate([jnp.broadcast_to(qs[s:s + 1], (SUBLANES, D_BRANCH)) for s in range(n_new)], axis=0)
        qbd = jnp.where(hmask, qb, 0.0).astype(BF16)
        kt = c_ref[0, 0:D_BRANCH, :].astype(BF16)
        vt = c_ref[0, D_BRANCH:KV_W, :].astype(BF16)
        kn = nbuf[g * KV_W:g * KV_W + D_BRANCH, :].astype(BF16)
        vn = nbuf[g * KV_W + D_BRANCH:(g + 1) * KV_W, :].astype(BF16)
        sc = jnp.dot(qbd, kt, preferred_element_type=F32)
        scn = jnp.dot(qbd, kn, preferred_element_type=F32)
        s_c = lax.broadcasted_iota(jnp.int32, (_QROWS, length), 0) // SUBLANES
        l_c = lax.broadcasted_iota(jnp.int32, (_QROWS, length), 1)
        valid_c = (l_c >= s_c) & (((l_c - s_c) & (dil - 1)) == 0)
        s_n = lax.broadcasted_iota(jnp.int32, (_QROWS, LANES), 0) // SUBLANES
        j_n = lax.broadcasted_iota(jnp.int32, (_QROWS, LANES), 1) - new0
        valid_n = (j_n >= 0) & (j_n <= s_n) & (((s_n - j_n) & (dil - 1)) == 0)
        sc = jnp.where(valid_c, sc, NEG_INF)
        scn = jnp.where(valid_n, scn, NEG_INF)
        m = jnp.maximum(jnp.max(sc, axis=-1, keepdims=True), jnp.max(scn, axis=-1, keepdims=True))
        p = jnp.exp(sc - m)
        pn = jnp.exp(scn - m)
        den = jnp.sum(p, axis=-1, keepdims=True) + jnp.sum(pn, axis=-1, keepdims=True)
        u = (lax.dot_general(p.astype(BF16), vt, _NT, preferred_element_type=F32)
             + lax.dot_general(pn.astype(BF16), vn, _NT, preferred_element_type=F32))
        us.append(u)
        dens.append(den)
        lses.append(m + jnp.log(den))

    mx = jnp.maximum(jnp.maximum(lses[0], lses[1]), lses[2])
    es = [jnp.exp(l - mx) for l in lses]
    tot = es[0] + es[1] + es[2]
    for g in range(N_BRANCH):
        coef = es[g] / tot / dens[g]
        a = jnp.where(hmask, us[g] * coef, 0.0)
        a = a.reshape(n_new, SUBLANES, D_BRANCH).sum(axis=1)
        att_ref[0, :, g * D_BRANCH:(g + 1) * D_BRANCH] = a

    lane_t = lax.broadcasted_iota(jnp.int32, (_SHIFT_ROWS, LANES), 1)
    keep = lane_t < new0
    for g, (c_ref, o_ref) in enumerate(zip((c1_ref, c4_ref, c16_ref), (o1_ref, o4_ref, o16_ref))):
        ntiles = c_ref.shape[2] // LANES

        def body(i, carry, g=g, c_ref=c_ref, o_ref=o_ref, ntiles=ntiles):
            r0 = pl.multiple_of(i * _SHIFT_ROWS, _SHIFT_ROWS)
            nxt = pltpu.roll(c_ref[0, pl.ds(r0, _SHIFT_ROWS), 0:LANES], new0, axis=1)
            for j in range(ntiles):
                cur = nxt
                if j + 1 < ntiles:
                    nxt = pltpu.roll(c_ref[0, pl.ds(r0, _SHIFT_ROWS), (j + 1) * LANES:(j + 2) * LANES], new0, axis=1)
                else:
                    nxt = nbuf[pl.ds(g * KV_W + r0, _SHIFT_ROWS), :]
                o_ref[0, pl.ds(r0, _SHIFT_ROWS), j * LANES:(j + 1) * LANES] = jnp.where(keep, cur, nxt)
            return carry

        lax.fori_loop(0, KV_W // _SHIFT_ROWS, body, 0)


def _sample_attn(q_s, new_t, caches_t):
    b, n_new, _ = q_s.shape
    per_tile = LANES // n_new
    cache_specs = [pl.BlockSpec((1, KV_W, c.shape[2]), lambda i: (i, 0, 0)) for c in caches_t]
    outs = pl.pallas_call(
        functools.partial(_sample_attn_kernel, n_new),
        grid=(b,),
        in_specs=[
            pl.BlockSpec((1, n_new, D_ATT), lambda i: (i, 0, 0)),
            pl.BlockSpec((N_BRANCH * KV_W, LANES), lambda i: (0, i // per_tile)),
        ] + cache_specs,
        out_specs=[pl.BlockSpec((1, n_new, D_ATT), lambda i: (i, 0, 0))] + cache_specs,
        out_shape=[jax.ShapeDtypeStruct((b, n_new, D_ATT), F32)]
        + [jax.ShapeDtypeStruct(c.shape, F32) for c in caches_t],
        scratch_shapes=[pltpu.VMEM((N_BRANCH * KV_W, LANES), F32)],
        compiler_params=pltpu.CompilerParams(dimension_semantics=("parallel",)),
        name="sample_attn_shift",
    )(q_s, new_t, *caches_t)
    return outs[0], outs[1:]


_GROUP_W = SSD_HEADS_PER_GROUP * SSD_HEAD_DIM
_STATE_W = 2 * LANES


def _split3(a):
    a1 = a.astype(BF16)
    r1 = a - a1.astype(F32)
    a2 = r1.astype(BF16)
    a3 = (r1 - a2.astype(F32)).astype(BF16)
    return jnp.concatenate([a1, a2, a3], axis=1)


def _ssd_kernel(valid_rows, xbc_ref, z_ref, dt_ref, conv0_ref, ssm0_ref, cw_ref, cb_ref,
                dtb_ref, alog_ref, dsk_ref, nrm_ref, y_ref, ssm_ref, tail, st, ybuf):
    c = pl.program_id(1)
    nc = pl.num_programs(1)
    cl = SSD_CHUNK

    @pl.when(c == 0)
    def _():
        tail[...] = conv0_ref[0]
        zpad = jnp.zeros((_STATE_W - _GROUP_W, SSD_STATE), F32)
        for g in range(SSD_GROUPS):
            s = jnp.concatenate([ssm0_ref[0, SSD_HEADS_PER_GROUP * g + r] for r in range(SSD_HEADS_PER_GROUP)]
                                + [zpad], axis=0)
            st[g] = s.T

    xcur = xbc_ref[0]
    prev = tail[...]
    row8 = lax.broadcasted_iota(jnp.int32, (SUBLANES, CONV_DIM), 0)
    xc = cb_ref[...] + cw_ref[CONV_WIDTH - 1:CONV_WIDTH, :] * xcur
    for s in range(1, CONV_WIDTH):
        sh = pltpu.roll(xcur, s, axis=0)
        head = jnp.where(row8 < s, pltpu.roll(prev, s, axis=0), sh[0:SUBLANES])
        sh = jnp.concatenate([head, sh[SUBLANES:]], axis=0)
        xc = xc + cw_ref[CONV_WIDTH - 1 - s:CONV_WIDTH - s, :] * sh
    tail[...] = xcur[cl - SUBLANES:cl]
    xc = xc * _sigmoid(xc)
    xs = xc[:, :D_INNER]
    bm = xc[:, D_INNER:D_INNER + SSD_GROUPS * SSD_STATE]
    cm = xc[:, D_INNER + SSD_GROUPS * SSD_STATE:]

    r_i = lax.broadcasted_iota(jnp.int32, (cl, cl), 0)
    c_i = lax.broadcasted_iota(jnp.int32, (cl, cl), 1)
    causal = r_i >= c_i
    dt = _softplus(dt_ref[0] + dtb_ref[...])
    if valid_rows < cl:
        dt = jnp.where(r_i < valid_rows, dt, 0.0)
    da = dt * (-jnp.exp(alog_ref[...]))
    hi = lax.Precision.HIGHEST
    acum = jnp.dot(causal.astype(F32), da, precision=hi, preferred_element_type=F32)
    wdec = jnp.exp(acum[cl - 1:cl, :] - acum) * dt
    eac = jnp.exp(acum)
    e_r = lax.broadcasted_iota(jnp.int32, (3 * LANES, D_INNER), 0)
    e_c = lax.broadcasted_iota(jnp.int32, (3 * LANES, D_INNER), 1)
    expand = (e_c // SSD_HEAD_DIM == e_r % LANES).astype(BF16)
    both = jnp.dot(jnp.concatenate([_split3(wdec), _split3(eac)], axis=0), expand, preferred_element_type=F32)
    xw = xs * both[:cl]
    eac_e = both[cl:]
    cd_e = eac_e[cl - 1:cl, :]
    acum_t = acum.T
    dt_t = dt.T

    yoffs = []
    for g in range(SSD_GROUPS):
        bg = bm[:, g * SSD_STATE:(g + 1) * SSD_STATE]
        cgb = cm[:, g * SSD_STATE:(g + 1) * SSD_STATE].astype(BF16)
        cb = lax.dot_general(cgb, bg.astype(BF16), _NT, preferred_element_type=F32)
        sg = st[g]
        yoffs.append(jnp.dot(cgb, sg.astype(BF16), preferred_element_type=F32)[:, :_GROUP_W])
        gsl = slice(g * _GROUP_W, (g + 1) * _GROUP_W)
        snew = jnp.dot(bg.T.astype(BF16), xw[:, gsl].astype(BF16), preferred_element_type=F32)
        st[g, :, 0:_GROUP_W] = cd_e[:, gsl] * sg[:, :_GROUP_W] + snew
        for r in range(SSD_HEADS_PER_GROUP):
            h = SSD_HEADS_PER_GROUP * g + r
            seg = acum[:, h:h + 1] - acum_t[h:h + 1, :]
            lmat = jnp.exp(jnp.where(causal, seg, NEG_INF))
            mm = cb * lmat * dt_t[h:h + 1, :]
            hsl = slice(h * SSD_HEAD_DIM, (h + 1) * SSD_HEAD_DIM)
            ybuf[:, hsl] = jnp.dot(mm.astype(BF16), xs[:, hsl].astype(BF16), preferred_element_type=F32)

    y = ybuf[...] + jnp.concatenate(yoffs, axis=1) * eac_e + dsk_ref[...] * xs
    z = z_ref[0]
    y = y * (z * _sigmoid(z))
    y_ref[0] = _rms(y, nrm_ref[...])

    @pl.when(c == nc - 1)
    def _():
        for g in range(SSD_GROUPS):
            t = st[g].T
            for r in range(SSD_HEADS_PER_GROUP):
                ssm_ref[0, SSD_HEADS_PER_GROUP * g + r] = t[r * SSD_HEAD_DIM:(r + 1) * SSD_HEAD_DIM, :]


def _ssd(xbc, z, dt, conv0, ssm0, params, valid_rows):
    b, t, _ = xbc.shape
    cl = SSD_CHUNK
    cw, cb, dtb, alog, dsk, nrm = params
    tok = lambda w: pl.BlockSpec((1, cl, w), lambda bi, ci: (bi, ci, 0))
    ssm_spec = pl.BlockSpec((1, SSD_HEADS, SSD_HEAD_DIM, SSD_STATE), lambda bi, ci: (bi, 0, 0, 0))
    return pl.pallas_call(
        functools.partial(_ssd_kernel, valid_rows),
        grid=(b, t // cl),
        in_specs=[
            tok(CONV_DIM), tok(D_INNER), tok(DT_PAD),
            pl.BlockSpec((1, SUBLANES, CONV_DIM), lambda bi, ci: (bi, 0, 0)),
            ssm_spec,
            _const_spec((CONV_WIDTH, CONV_DIM)), _const_spec((1, CONV_DIM)),
            _const_spec((1, DT_PAD)), _const_spec((1, DT_PAD)),
            _const_spec((1, D_INNER)), _const_spec((1, D_INNER)),
        ],
        out_specs=[tok(D_INNER), ssm_spec],
        out_shape=[jax.ShapeDtypeStruct((b, t, D_INNER), F32),
                   jax.ShapeDtypeStruct((b, SSD_HEADS, SSD_HEAD_DIM, SSD_STATE), F32)],
        scratch_shapes=[pltpu.VMEM((SUBLANES, CONV_DIM), F32),
                        pltpu.VMEM((SSD_GROUPS, SSD_STATE, _STATE_W), F32),
                        pltpu.VMEM((cl, D_INNER), F32)],
        compiler_params=pltpu.CompilerParams(dimension_semantics=("parallel", "arbitrary")),
        name="ssd",
    )(xbc, z, dt, conv0, ssm0, cw, cb, dtb, alog, dsk, nrm)


def _load_tokens(ref, scr, dil):
    if dil == 1:
        return ref[0]
    n = ref.shape[1]
    halves = range(D_BRANCH // LANES)
    for r in range(dil):
        for half in halves:
            c0 = r * D_BRANCH + half * LANES
            scr[half, pl.ds(r, n, stride=dil), :] = ref[0, :, c0:c0 + LANES]
    return jnp.concatenate([scr[half] for half in halves], axis=1)


def _ffn_tail(h, gffn_ref, wg_ref, wu_ref, wd_ref, gfin_ref, y_ref, act_ref, final):
    hn = _rms(h, gffn_ref[...]).astype(BF16)
    for c in range(0, D_FF, FF_CHUNK):
        gate = jnp.dot(hn, wg_ref[:, c:c + FF_CHUNK], preferred_element_type=F32)
        up = jnp.dot(hn, wu_ref[:, c:c + FF_CHUNK], preferred_element_type=F32)
        act_ref[:, c:c + FF_CHUNK] = (gate * _sigmoid(gate) * up).astype(BF16)
    y = h + jnp.dot(act_ref[...], wd_ref[...], preferred_element_type=F32)
    y_ref[...] = _rms(y, gfin_ref[...]) if final else y


def _out_ffn_prompt_kernel(final, x_ref, *refs):
    att_refs = refs[:2 * N_BRANCH]
    ssd_ref, wo_ref, gffn_ref, wg_ref, wu_ref, wd_ref, gfin_ref, y_ref, act_ref = refs[2 * N_BRANCH:-2 * N_BRANCH]
    scrs = refs[-2 * N_BRANCH:]
    os_, ls = [], []
    for g, (win, dil) in enumerate(ATT_BRANCHES):
        os_.append(_load_tokens(att_refs[2 * g], scrs[2 * g], dil))
        ls.append(_load_tokens(att_refs[2 * g + 1], scrs[2 * g + 1], dil))
    mx = jnp.maximum(jnp.maximum(ls[0], ls[1]), ls[2])
    es = [jnp.exp(l - mx) for l in ls]
    inv = 1.0 / (es[0] + es[1] + es[2])
    h = x_ref[...]
    for g in range(N_BRANCH):
        a = (os_[g] * (es[g] * inv)).astype(BF16)
        h = h + jnp.dot(a, wo_ref[g * D_BRANCH:(g + 1) * D_BRANCH, :], preferred_element_type=F32)
    h = h + jnp.dot(ssd_ref[...].astype(BF16), wo_ref[D_ATT:D_MIX, :], preferred_element_type=F32)
    _ffn_tail(h, gffn_ref, wg_ref, wu_ref, wd_ref, gfin_ref, y_ref, act_ref, final)


def _out_ffn_sample_kernel(final, x_ref, att_ref, ssd_ref, wo_ref, gffn_ref, wg_ref, wu_ref, wd_ref, gfin_ref,
                           y_ref, act_ref):
    h = x_ref[...]
    h = h + jnp.dot(att_ref[...].astype(BF16), wo_ref[0:D_ATT, :], preferred_element_type=F32)
    h = h + jnp.dot(ssd_ref[...].astype(BF16), wo_ref[D_ATT:D_MIX, :], preferred_element_type=F32)
    _ffn_tail(h, gffn_ref, wg_ref, wu_ref, wd_ref, gfin_ref, y_ref, act_ref, final)


def _ffn_weight_specs():
    return [_const_spec((D_MIX, D_MODEL)), _const_spec((1, D_MODEL)),
            _const_spec((D_MODEL, D_FF)), _const_spec((D_MODEL, D_FF)),
            _const_spec((D_FF, D_MODEL)), _const_spec((1, D_MODEL))]


def _out_ffn_prompt(x2d, b, t, att, ssd2d, weights, final):
    m = b * t
    tm = FFN_TM
    tpb = t // tm
    tok = lambda w: pl.BlockSpec((tm, w), lambda i: (i, 0))
    att_specs = []
    for win, dil in ATT_BRANCHES:
        att_specs += [pl.BlockSpec((1, tm // dil, dil * D_BRANCH), lambda i: (i // tpb, i % tpb, 0))] * 2
    return pl.pallas_call(
        functools.partial(_out_ffn_prompt_kernel, final),
        grid=(m // tm,),
        in_specs=[tok(D_MODEL)] + att_specs + [tok(D_INNER)] + _ffn_weight_specs(),
        out_specs=tok(D_MODEL),
        out_shape=jax.ShapeDtypeStruct((m, D_MODEL), F32),
        scratch_shapes=[pltpu.VMEM((tm, D_FF), BF16)]
        + [pltpu.VMEM((D_BRANCH // LANES, tm, LANES), F32)] * (2 * N_BRANCH),
        compiler_params=pltpu.CompilerParams(dimension_semantics=("parallel",)),
        name="out_ffn_prompt",
    )(x2d, *att, ssd2d, *weights)


def _out_ffn_sample(x2d, att2d, ssd2d, weights, final):
    m = x2d.shape[0]
    tm = m // 2
    tok = lambda w: pl.BlockSpec((tm, w), lambda i: (i, 0))
    return pl.pallas_call(
        functools.partial(_out_ffn_sample_kernel, final),
        grid=(m // tm,),
        in_specs=[tok(D_MODEL), tok(D_ATT), tok(D_INNER)] + _ffn_weight_specs(),
        out_specs=tok(D_MODEL),
        out_shape=jax.ShapeDtypeStruct((m, D_MODEL), F32),
        scratch_shapes=[pltpu.VMEM((tm, D_FF), BF16)],
        compiler_params=pltpu.CompilerParams(dimension_semantics=("parallel",)),
        name="out_ffn_sample",
    )(x2d, att2d, ssd2d, *weights)


def _pad_lanes(v, width):
    return jnp.pad(v, ((0, 0), (0, width - v.shape[1])))


def kernel(x_prompt, x_sample, cache_kv_d1, cache_kv_d4, cache_kv_d16, state_conv, state_ssm, norm_mix, w_in, conv_w, conv_b, dt_bias, a_log, d_skip, norm_ssd, w_out, norm_ffn, w_gate_up, w_down, norm_final):
    bp, tp, _ = x_prompt.shape
    bs, ts, _ = x_sample.shape
    depth = w_in.shape[0]
    caches = (cache_kv_d1, cache_kv_d4, cache_kv_d16)
    hp, hs = x_prompt.reshape(bp * tp, D_MODEL), x_sample.reshape(bs * ts, D_MODEL)
    gfin = norm_final.reshape(1, D_MODEL)
    p_new, s_new = [[] for _ in range(5)], [[] for _ in range(5)]

    for l in range(depth):
        wi = w_in[l]
        wk, wv = wi[:, D_ATT:2 * D_ATT], wi[:, 2 * D_ATT:3 * D_ATT]
        wq = wi[:, :D_ATT].astype(BF16)
        wkv = jnp.concatenate([w[:, g * D_BRANCH:(g + 1) * D_BRANCH] for g in range(N_BRANCH) for w in (wk, wv)],
                              axis=1).astype(BF16)
        o = 3 * D_ATT
        wzx = wi[:, o:o + D_INNER + CONV_DIM].astype(BF16)
        wdt = _pad_lanes(wi[:, o + D_INNER + CONV_DIM:], DT_PAD).astype(BF16)
        gmix = norm_mix[l].reshape(1, D_MODEL)
        ssd_params = (conv_w[l], conv_b[l].reshape(1, CONV_DIM),
                      _pad_lanes(dt_bias[l].reshape(1, SSD_HEADS), DT_PAD),
                      _pad_lanes(a_log[l].reshape(1, SSD_HEADS), DT_PAD),
                      jnp.repeat(d_skip[l], SSD_HEAD_DIM).reshape(1, D_INNER),
                      norm_ssd[l].reshape(1, D_INNER))
        ffn_w = (w_out[l].astype(BF16), norm_ffn[l].reshape(1, D_MODEL),
                 w_gate_up[l][:, :D_FF].astype(BF16), w_gate_up[l][:, D_FF:].astype(BF16),
                 w_down[l].astype(BF16), gfin)
        final = l == depth - 1

        qkv, kts, (z, xbc, dt) = _in_proj_prompt(hp, bp, tp, gmix, wq, wkv, wzx, wdt)
        for g, (win, dil) in enumerate(ATT_BRANCHES):
            kt = kts[g].reshape(bp, 2, HEADS_PER_BRANCH, HEAD_DIM, win)
            p_new[g].append(jnp.transpose(kt, (0, 4, 1, 2, 3)))
        att = _attn_prompt(qkv, bp, tp)
        xbc3 = xbc.reshape(bp, tp, CONV_DIM)
        conv0 = jnp.zeros((bp, SUBLANES, CONV_DIM), F32)
        ssm0 = jnp.zeros((bp, SSD_HEADS, SSD_HEAD_DIM, SSD_STATE), F32)
        y_ssd, ssm_p = _ssd(xbc3, z.reshape(bp, tp, D_INNER), dt.reshape(bp, tp, DT_PAD), conv0, ssm0,
                            ssd_params, SSD_CHUNK)
        p_new[3].append(xbc3[:, tp - (CONV_WIDTH - 1):])
        p_new[4].append(ssm_p)
        hp = _out_ffn_prompt(hp, bp, tp, att, y_ssd.reshape(bp * tp, D_INNER), ffn_w, final)

        q, z, xbc, dt, new_t = _in_proj_sample(hs, gmix, wq, wzx, wdt, wkv.T)
        caches_t = [jnp.transpose(c[l], (0, 2, 3, 4, 1)).reshape(bs, KV_W, c.shape[2]) for c in caches]
        att_s, shifted = _sample_attn(q.reshape(bs, ts, D_ATT), new_t, caches_t)
        for g in range(N_BRANCH):
            lg = shifted[g].shape[2]
            s_new[g].append(jnp.transpose(shifted[g].reshape(bs, 2, HEADS_PER_BRANCH, HEAD_DIM, lg), (0, 4, 1, 2, 3)))
        pad_t = lambda a: jnp.pad(a.reshape(bs, ts, a.shape[1]), ((0, 0), (0, SSD_CHUNK - ts), (0, 0)))
        xbc3 = xbc.reshape(bs, ts, CONV_DIM)
        conv0 = jnp.pad(state_conv[l], ((0, 0), (SUBLANES - (CONV_WIDTH - 1), 0), (0, 0)))
        y_ssd, ssm_s = _ssd(pad_t(xbc), pad_t(z), pad_t(dt), conv0, state_ssm[l], ssd_params, ts)
        s_new[3].append(jnp.concatenate([state_conv[l], xbc3], axis=1)[:, -(CONV_WIDTH - 1):])
        s_new[4].append(ssm_s)
        hs = _out_ffn_sample(hs, att_s.reshape(bs * ts, D_ATT), y_ssd[:, :ts].reshape(bs * ts, D_INNER), ffn_w, final)

    y_prompt = hp.reshape(bp, tp, D_MODEL)
    y_sample = hs.reshape(bs, ts, D_MODEL)
    p_out = [jnp.stack(a) for a in p_new]
    s_out = [jnp.stack(a) for a in s_new]
    return (y_prompt, y_sample, *p_out, *s_out)
```

```python
import functools

import jax
import jax.numpy as jnp
from jax import lax
from jax.experimental import pallas as pl
from jax.experimental.pallas import tpu as pltpu

F32 = jnp.float32
BF16 = jnp.bfloat16

D_MODEL = 1024
HEAD_DIM = 64
ATT_BRANCHES = ((128, 1), (512, 4), (2048, 16))
N_BRANCH = 3
HEADS_PER_BRANCH = 4
D_BRANCH = HEADS_PER_BRANCH * HEAD_DIM
D_ATT = N_BRANCH * D_BRANCH
SSD_HEADS = 12
SSD_HEAD_DIM = 64
D_INNER = SSD_HEADS * SSD_HEAD_DIM
SSD_GROUPS = 4
SSD_HEADS_PER_GROUP = 3
SSD_STATE = 128
CONV_WIDTH = 4
CONV_DIM = D_INNER + 2 * SSD_GROUPS * SSD_STATE
SSD_CHUNK = 128
D_MIX = D_ATT + D_INNER
D_FF = 2816
RMS_EPS = 1e-5
NEG_INF = -1e30
ATT_SCALE = HEAD_DIM ** -0.5
BAND = 128

LANES = 128
SUBLANES = 8
KV_W = 2 * D_BRANCH
DT_PAD = LANES
FF_CHUNK = 256
PROJ_CHUNK = 256
PROJ_TM = 512
ATT_QB = 2 * BAND
FFN_TM = 512

_NT = (((1,), (1,)), ((), ()))


def _rms(x, g):
    inv = lax.rsqrt(jnp.mean(x * x, axis=-1, keepdims=True) + RMS_EPS)
    return x * inv * g


def _sigmoid(x):
    return 1.0 / (1.0 + jnp.exp(-x))


def _softplus(x):
    return jnp.maximum(x, 0.0) + jnp.log1p(jnp.exp(-jnp.abs(x)))


def _const_spec(shape):
    nd = len(shape)
    return pl.BlockSpec(shape, lambda *_: (0,) * nd, pipeline_mode=pl.Buffered(1))


def _proj_store(xn, w_ref, out_ref, width):
    for c in range(0, width, PROJ_CHUNK):
        w = min(PROJ_CHUNK, width - c)
        out_ref[:, c:c + w] = jnp.dot(xn, w_ref[:, c:c + w], preferred_element_type=F32)


_Q_TILES = D_ATT // LANES
_KV_TILES = KV_W // LANES
_HALVES = D_BRANCH // LANES


def _in_proj_prompt_kernel(tiles_per_batch, x_ref, g_ref, wq_ref, wkv_ref, wzx_ref, wdt_ref, *refs):
    qkv_refs = refs[:3 * N_BRANCH]
    kt_refs = refs[3 * N_BRANCH:4 * N_BRANCH]
    z_ref, xbc_ref, dt_ref, acc = refs[4 * N_BRANCH:]
    tm = x_ref.shape[0]
    ti = lax.rem(pl.program_id(0), tiles_per_batch)
    xn = _rms(x_ref[...], g_ref[...]).astype(BF16)

    for c in range(0, D_ATT, PROJ_CHUNK):
        res = jnp.dot(xn, wq_ref[:, c:c + PROJ_CHUNK], preferred_element_type=F32) * ATT_SCALE
        for j in range(PROJ_CHUNK // LANES):
            acc[c // LANES + j] = res[:, j * LANES:(j + 1) * LANES]
    for c in range(0, N_BRANCH * KV_W, PROJ_CHUNK):
        res = jnp.dot(xn, wkv_ref[:, c:c + PROJ_CHUNK], preferred_element_type=F32)
        for j in range(PROJ_CHUNK // LANES):
            acc[_Q_TILES + c // LANES + j] = res[:, j * LANES:(j + 1) * LANES]
    _proj_store(xn, wzx_ref.at[:, 0:D_INNER], z_ref, D_INNER)
    _proj_store(xn, wzx_ref.at[:, D_INNER:D_INNER + CONV_DIM], xbc_ref, CONV_DIM)
    _proj_store(xn, wdt_ref, dt_ref, DT_PAD)

    for g, (win, dil) in enumerate(ATT_BRANCHES):
        n = tm // dil
        tiles = (_HALVES * g, _Q_TILES + _KV_TILES * g, _Q_TILES + _KV_TILES * g + _HALVES)
        for out_ref, t0 in zip(qkv_refs[3 * g:3 * g + 3], tiles):
            for r in range(dil):
                for half in range(_HALVES):
                    c0 = r * D_BRANCH + half * LANES
                    rows = acc[t0 + half] if dil == 1 else acc[t0 + half, pl.ds(r, n, stride=dil), :]
                    out_ref[0, :, c0:c0 + LANES] = rows.astype(BF16)

    for g, (win, dil) in enumerate(ATT_BRANCHES):
        w = kt_refs[g].shape[2]
        first = tiles_per_batch - win // w

        @pl.when(ti >= first)
        def _(g=g, w=w):
            for j in range(_KV_TILES):
                kt_refs[g][0, j * LANES:(j + 1) * LANES, :] = acc[_Q_TILES + _KV_TILES * g + j, tm - w:tm, :].T


def _in_proj_prompt(x2d, b, t, g, wq, wkv, wzx, wdt):
    m = b * t
    tm = PROJ_TM
    tpb = t // tm
    tok = lambda w: pl.BlockSpec((tm, w), lambda i: (i, 0))
    qkv_specs, qkv_shapes, kt_specs, kt_shapes = [], [], [], []
    for win, dil in ATT_BRANCHES:
        spec = pl.BlockSpec((1, tm // dil, dil * D_BRANCH), lambda i: (i // tpb, i % tpb, 0))
        shape = jax.ShapeDtypeStruct((b, t // dil, dil * D_BRANCH), BF16)
        qkv_specs += [spec] * 3
        qkv_shapes += [shape] * 3
        w = min(win, tm)
        first = tpb - win // w
        kt_specs.append(pl.BlockSpec((1, KV_W, w), lambda i, first=first: (i // tpb, 0, jnp.maximum(i % tpb - first, 0))))
        kt_shapes.append(jax.ShapeDtypeStruct((b, KV_W, win), F32))
    outs = pl.pallas_call(
        functools.partial(_in_proj_prompt_kernel, tpb),
        grid=(m // tm,),
        in_specs=[tok(D_MODEL), _const_spec((1, D_MODEL)), _const_spec((D_MODEL, D_ATT)),
                  _const_spec((D_MODEL, N_BRANCH * KV_W)), _const_spec((D_MODEL, D_INNER + CONV_DIM)),
                  _const_spec((D_MODEL, DT_PAD))],
        out_specs=qkv_specs + kt_specs + [tok(D_INNER), tok(CONV_DIM), tok(DT_PAD)],
        out_shape=qkv_shapes + kt_shapes + [jax.ShapeDtypeStruct((m, w), F32) for w in (D_INNER, CONV_DIM, DT_PAD)],
        scratch_shapes=[pltpu.VMEM((_Q_TILES + N_BRANCH * _KV_TILES, tm, LANES), F32)],
        compiler_params=pltpu.CompilerParams(dimension_semantics=("arbitrary",)),
        name="in_proj_prompt",
    )(x2d, g, wq, wkv, wzx, wdt)
    n = 3 * N_BRANCH
    return outs[:n], outs[n:n + N_BRANCH], outs[n + N_BRANCH:]


def _in_proj_sample_kernel(x_ref, g_ref, wq_ref, wxbc_ref, wkvt_ref, wzxt_ref, wdtt_ref,
                           q_ref, xbc_ref, kvt_ref, zxt_ref, dtt_ref):
    xn = _rms(x_ref[...], g_ref[...]).astype(BF16)
    for c in range(0, D_ATT, PROJ_CHUNK):
        q = jnp.dot(xn, wq_ref[:, c:c + PROJ_CHUNK], preferred_element_type=F32) * ATT_SCALE
        q_ref[:, c:c + PROJ_CHUNK] = q.astype(BF16).astype(F32)
    _proj_store(xn, wxbc_ref, xbc_ref, CONV_DIM)
    for wt_ref, out_ref in ((wkvt_ref, kvt_ref), (wzxt_ref, zxt_ref), (wdtt_ref, dtt_ref)):
        for r in range(0, wt_ref.shape[0], PROJ_CHUNK):
            n = min(PROJ_CHUNK, wt_ref.shape[0] - r)
            out_ref[r:r + n, :] = lax.dot_general(wt_ref[r:r + n, :], xn, _NT, preferred_element_type=F32)


def _in_proj_sample(x2d, g, wq, wxbc, wkvt, wzxt, wdtt):
    m = x2d.shape[0]
    ins = (x2d, g, wq, wxbc, wkvt, wzxt, wdtt)
    shapes = [(m, D_ATT), (m, CONV_DIM), (wkvt.shape[0], m), (wzxt.shape[0], m), (wdtt.shape[0], m)]
    return pl.pallas_call(
        _in_proj_sample_kernel,
        grid=(1,),
        in_specs=[_const_spec(a.shape) for a in ins],
        out_specs=[pl.BlockSpec(s, lambda i: (0, 0)) for s in shapes],
        out_shape=[jax.ShapeDtypeStruct(s, F32) for s in shapes],
        name="in_proj_sample",
    )(*ins)


_ATT_STEPS = 16
_ATT_ROWS = 128


def _attn_prompt_kernel(*refs):
    ins, outs = refs[:5 * N_BRANCH], refs[5 * N_BRANCH:]
    j = pl.program_id(1)
    qi = lax.broadcasted_iota(jnp.int32, (BAND, 2 * BAND), 0)
    kj = lax.broadcasted_iota(jnp.int32, (BAND, 2 * BAND), 1)
    dist = BAND + qi - kj
    in_band = (dist >= 0) & (dist <= BAND)

    tasks = []
    for g, (win, dil) in enumerate(ATT_BRANCHES):
        q_ref, kp_ref, kc_ref, vp_ref, vc_ref = ins[5 * g:5 * g + 5]
        i = lax.rem(j, _ATT_STEPS // dil)
        kc, vc = kc_ref[0], vc_ref[0]
        for sb in range(ATT_QB // BAND):
            rows = slice(sb * BAND, (sb + 1) * BAND)
            if sb == 0:
                k = jnp.concatenate([kp_ref[0], kc[rows]], axis=0)
                v = jnp.concatenate([vp_ref[0], vc[rows]], axis=0)
                valid = in_band & ((kj >= BAND) | (i > 0))
            else:
                k = kc[(sb - 1) * BAND:(sb + 1) * BAND]
                v = vc[(sb - 1) * BAND:(sb + 1) * BAND]
                valid = in_band
            for h in range(HEADS_PER_BRANCH):
                sl = slice(h * HEAD_DIM, (h + 1) * HEAD_DIM)
                for r0 in range(0, BAND, _ATT_ROWS):
                    sub = slice(sb * BAND + r0, sb * BAND + r0 + _ATT_ROWS)
                    tasks.append((q_ref[0, sub, sl], k[:, sl], v[:, sl], valid[r0:r0 + _ATT_ROWS],
                                  outs[2 * g], outs[2 * g + 1], sub, sl))

    ss = [jnp.where(t[3], lax.dot_general(t[0], t[1], _NT, preferred_element_type=F32), NEG_INF) for t in tasks]
    ms = [jnp.max(s, axis=-1, keepdims=True) for s in ss]
    ps = [jnp.exp(s - m) for s, m in zip(ss, ms)]
    dens = [jnp.sum(p, axis=-1, keepdims=True) for p in ps]
    nums = [jnp.dot(p.astype(BF16), t[2], preferred_element_type=F32) for p, t in zip(ps, tasks)]
    for t, m, den, num in zip(tasks, ms, dens, nums):
        o_ref, l_ref, rows, sl = t[4:]
        o_ref[0, rows, sl] = num / den
        l_ref[0, rows, sl] = jnp.broadcast_to(m + jnp.log(den), (_ATT_ROWS, HEAD_DIM))


def _attn_prompt(qkv, b, t):
    assert t // ATT_QB == _ATT_STEPS
    in_specs, args, out_specs, out_shapes = [], [], [], []
    for g, (win, dil) in enumerate(ATT_BRANCHES):
        nblk = _ATT_STEPS // dil
        cur = pl.BlockSpec((1, ATT_QB, D_BRANCH), lambda bi, j, nblk=nblk: (bi, j % nblk, j // nblk))
        prv = pl.BlockSpec((1, BAND, D_BRANCH),
                           lambda bi, j, nblk=nblk: (bi, jnp.maximum((ATT_QB // BAND) * (j % nblk) - 1, 0), j // nblk))
        q, k, v = qkv[3 * g:3 * g + 3]
        in_specs += [cur, prv, cur, prv, cur]
        args += [q, k, k, v, v]
        out_specs += [cur, cur]
        out_shapes += [jax.ShapeDtypeStruct((b, t // dil, dil * D_BRANCH), F32)] * 2
    return pl.pallas_call(
        _attn_prompt_kernel,
        grid=(b, _ATT_STEPS),
        in_specs=in_specs,
        out_specs=out_specs,
        out_shape=out_shapes,
        compiler_params=pltpu.CompilerParams(dimension_semantics=("parallel", "arbitrary")),
        name="attn_prompt",
    )(*args)


_QROWS = 4 * SUBLANES
_SHIFT_ROWS = 64


def _sample_attn_kernel(n_new, q_ref, nt_ref, c1_ref, c4_ref, c16_ref,
                        att_ref, o1_ref, o4_ref, o16_ref, nbuf):
    b = pl.program_id(0)
    per_tile = LANES // n_new
    new0 = LANES - n_new
    shift = new0 - n_new * lax.rem(b, per_tile)
    nbuf[...] = pltpu.roll(nt_ref[...], shift, axis=1)

    q = q_ref[0]
    row = lax.broadcasted_iota(jnp.int32, (_QROWS, D_BRANCH), 0)
    lane = lax.broadcasted_iota(jnp.int32, (_QROWS, D_BRANCH), 1)
    hmask = (lane // HEAD_DIM) == (row % SUBLANES)

    us, dens, lses = [], [], []
    for g, ((win, dil), c_ref) in enumerate(zip(ATT_BRANCHES, (c1_ref, c4_ref, c16_ref))):
        length = c_ref.shape[2]
        qs = q[:, g * D_BRANCH:(g + 1) * D_BRANCH]
        qb = jnp.concatenate([jnp.broadcast_to(qs[s:s + 1], (SUBLANES, D_BRANCH)) for s in range(n_new)], axis=0)
        qbd = jnp.where(hmask, qb, 0.0).astype(BF16)
        kt = c_ref[0, 0:D_BRANCH, :].astype(BF16)
        vt = c_ref[0, D_BRANCH:KV_W, :].astype(BF16)
        kn = nbuf[g * KV_W:g * KV_W + D_BRANCH, :].astype(BF16)
        vn = nbuf[g * KV_W + D_BRANCH:(g + 1) * KV_W, :].astype(BF16)
        sc = jnp.dot(qbd, kt, preferred_element_type=F32)
        scn = jnp.dot(qbd, kn, preferred_element_type=F32)
        s_c = lax.broadcasted_iota(jnp.int32, (_QROWS, length), 0) // SUBLANES
        l_c = lax.broadcasted_iota(jnp.int32, (_QROWS, length), 1)
        valid_c = (l_c >= s_c) & (((l_c - s_c) & (dil - 1)) == 0)
        s_n = lax.broadcasted_iota(jnp.int32, (_QROWS, LANES), 0) // SUBLANES
        j_n = lax.broadcasted_iota(jnp.int32, (_QROWS, LANES), 1) - new0
        valid_n = (j_n >= 0) & (j_n <= s_n) & (((s_n - j_n) & (dil - 1)) == 0)
        sc = jnp.where(valid_c, sc, NEG_INF)
        scn = jnp.where(valid_n, scn, NEG_INF)
        m = jnp.maximum(jnp.max(sc, axis=-1, keepdims=True), jnp.max(scn, axis=-1, keepdims=True))
        p = jnp.exp(sc - m)
        pn = jnp.exp(scn - m)
        den = jnp.sum(p, axis=-1, keepdims=True) + jnp.sum(pn, axis=-1, keepdims=True)
        u = (lax.dot_general(p.astype(BF16), vt, _NT, preferred_element_type=F32)
             + lax.dot_general(pn.astype(BF16), vn, _NT, preferred_element_type=F32))
        us.append(u)
        dens.append(den)
        lses.append(m + jnp.log(den))

    mx = jnp.maximum(jnp.maximum(lses[0], lses[1]), lses[2])
    es = [jnp.exp(l - mx) for l in lses]
    tot = es[0] + es[1] + es[2]
    for g in range(N_BRANCH):
        coef = es[g] / tot / dens[g]
        a = jnp.where(hmask, us[g] * coef, 0.0)
        a = a.reshape(n_new, SUBLANES, D_BRANCH).sum(axis=1)
        att_ref[0, :, g * D_BRANCH:(g + 1) * D_BRANCH] = a

    lane_t = lax.broadcasted_iota(jnp.int32, (_SHIFT_ROWS, LANES), 1)
    keep = lane_t < new0
    for g, (c_ref, o_ref) in enumerate(zip((c1_ref, c4_ref, c16_ref), (o1_ref, o4_ref, o16_ref))):
        ntiles = c_ref.shape[2] // LANES
        for r0 in range(0, KV_W, _SHIFT_ROWS):
            rows = slice(r0, r0 + _SHIFT_ROWS)
            nxt = pltpu.roll(c_ref[0, rows, 0:LANES], new0, axis=1)
            for j in range(ntiles):
                cur = nxt
                if j + 1 < ntiles:
                    nxt = pltpu.roll(c_ref[0, rows, (j + 1) * LANES:(j + 2) * LANES], new0, axis=1)
                else:
                    nxt = nbuf[g * KV_W + r0:g * KV_W + r0 + _SHIFT_ROWS, :]
                o_ref[0, rows, j * LANES:(j + 1) * LANES] = jnp.where(keep, cur, nxt)


def _sample_attn(q_s, new_t, caches_t):
    b, n_new, _ = q_s.shape
    per_tile = LANES // n_new
    cache_specs = [pl.BlockSpec((1, KV_W, c.shape[2]), lambda i: (i, 0, 0)) for c in caches_t]
    outs = pl.pallas_call(
        functools.partial(_sample_attn_kernel, n_new),
        grid=(b,),
        in_specs=[
            pl.BlockSpec((1, n_new, D_ATT), lambda i: (i, 0, 0)),
            pl.BlockSpec((N_BRANCH * KV_W, LANES), lambda i: (0, i // per_tile)),
        ] + cache_specs,
        out_specs=[pl.BlockSpec((1, n_new, D_ATT), lambda i: (i, 0, 0))] + cache_specs,
        out_shape=[jax.ShapeDtypeStruct((b, n_new, D_ATT), F32)]
        + [jax.ShapeDtypeStruct(c.shape, F32) for c in caches_t],
        scratch_shapes=[pltpu.VMEM((N_BRANCH * KV_W, LANES), F32)],
        compiler_params=pltpu.CompilerParams(dimension_semantics=("parallel",)),
        name="sample_attn_shift",
    )(q_s, new_t, *caches_t)
    return outs[0], outs[1:]


_GROUP_W = SSD_HEADS_PER_GROUP * SSD_HEAD_DIM
_STATE_W = 2 * LANES


def _split3(a):
    a1 = a.astype(BF16)
    r1 = a - a1.astype(F32)
    a2 = r1.astype(BF16)
    a3 = (r1 - a2.astype(F32)).astype(BF16)
    return jnp.concatenate([a1, a2, a3], axis=1)


def _ssd_kernel(xbc_ref, z_ref, dt_ref, conv0_ref, ssm0_ref, cw_ref, cb_ref,
                dtb_ref, alog_ref, dsk_ref, nrm_ref, y_ref, ssm_ref, tail, st, ybuf):
    c = pl.program_id(1)
    nc = pl.num_programs(1)
    cl = SSD_CHUNK

    @pl.when(c == 0)
    def _():
        tail[...] = conv0_ref[0]
        zpad = jnp.zeros((_STATE_W - _GROUP_W, SSD_STATE), F32)
        for g in range(SSD_GROUPS):
            s = jnp.concatenate([ssm0_ref[0, SSD_HEADS_PER_GROUP * g + r] for r in range(SSD_HEADS_PER_GROUP)]
                                + [zpad], axis=0)
            st[g] = s.T

    xcur = xbc_ref[0]
    prev = tail[...]
    row8 = lax.broadcasted_iota(jnp.int32, (SUBLANES, CONV_DIM), 0)
    xc = cb_ref[...] + cw_ref[CONV_WIDTH - 1:CONV_WIDTH, :] * xcur
    for s in range(1, CONV_WIDTH):
        sh = pltpu.roll(xcur, s, axis=0)
        head = jnp.where(row8 < s, pltpu.roll(prev, s, axis=0), sh[0:SUBLANES])
        sh = jnp.concatenate([head, sh[SUBLANES:]], axis=0)
        xc = xc + cw_ref[CONV_WIDTH - 1 - s:CONV_WIDTH - s, :] * sh
    tail[...] = xcur[cl - SUBLANES:cl]
    xc = xc * _sigmoid(xc)
    xs = xc[:, :D_INNER]
    bm = xc[:, D_INNER:D_INNER + SSD_GROUPS * SSD_STATE]
    cm = xc[:, D_INNER + SSD_GROUPS * SSD_STATE:]

    r_i = lax.broadcasted_iota(jnp.int32, (cl, cl), 0)
    c_i = lax.broadcasted_iota(jnp.int32, (cl, cl), 1)
    causal = r_i >= c_i
    dt = _softplus(dt_ref[0] + dtb_ref[...])
    da = dt * (-jnp.exp(alog_ref[...]))
    hi = lax.Precision.HIGHEST
    acum = jnp.dot(causal.astype(F32), da, precision=hi, preferred_element_type=F32)
    wdec = jnp.exp(acum[cl - 1:cl, :] - acum) * dt
    eac = jnp.exp(acum)
    e_r = lax.broadcasted_iota(jnp.int32, (3 * LANES, D_INNER), 0)
    e_c = lax.broadcasted_iota(jnp.int32, (3 * LANES, D_INNER), 1)
    expand = (e_c // SSD_HEAD_DIM == e_r % LANES).astype(BF16)
    both = jnp.dot(jnp.concatenate([_split3(wdec), _split3(eac)], axis=0), expand, preferred_element_type=F32)
    xw = xs * both[:cl]
    eac_e = both[cl:]
    cd_e = eac_e[cl - 1:cl, :]
    acum_t = acum.T
    dt_t = dt.T

    yoffs = []
    for g in range(SSD_GROUPS):
        bg = bm[:, g * SSD_STATE:(g + 1) * SSD_STATE]
        cgb = cm[:, g * SSD_STATE:(g + 1) * SSD_STATE].astype(BF16)
        cb = lax.dot_general(cgb, bg.astype(BF16), _NT, preferred_element_type=F32)
        sg = st[g]
        yoffs.append(jnp.dot(cgb, sg.astype(BF16), preferred_element_type=F32)[:, :_GROUP_W])
        gsl = slice(g * _GROUP_W, (g + 1) * _GROUP_W)
        snew = jnp.dot(bg.T.astype(BF16), xw[:, gsl].astype(BF16), preferred_element_type=F32)
        st[g, :, 0:_GROUP_W] = cd_e[:, gsl] * sg[:, :_GROUP_W] + snew
        for r in range(SSD_HEADS_PER_GROUP):
            h = SSD_HEADS_PER_GROUP * g + r
            seg = acum[:, h:h + 1] - acum_t[h:h + 1, :]
            lmat = jnp.exp(jnp.where(causal, seg, NEG_INF))
            mm = cb * lmat * dt_t[h:h + 1, :]
            hsl = slice(h * SSD_HEAD_DIM, (h + 1) * SSD_HEAD_DIM)
            ybuf[:, hsl] = jnp.dot(mm.astype(BF16), xs[:, hsl].astype(BF16), preferred_element_type=F32)

    y = ybuf[...] + jnp.concatenate(yoffs, axis=1) * eac_e + dsk_ref[...] * xs
    z = z_ref[0]
    y = y * (z * _sigmoid(z))
    y_ref[0] = _rms(y, nrm_ref[...])

    @pl.when(c == nc - 1)
    def _():
        for g in range(SSD_GROUPS):
            t = st[g].T
            for r in range(SSD_HEADS_PER_GROUP):
                ssm_ref[0, SSD_HEADS_PER_GROUP * g + r] = t[r * SSD_HEAD_DIM:(r + 1) * SSD_HEAD_DIM, :]


def _ssd(xbc, z, dt, conv0, ssm0, params):
    b, t, _ = xbc.shape
    cl = SSD_CHUNK
    cw, cb, dtb, alog, dsk, nrm = params
    tok = lambda w: pl.BlockSpec((1, cl, w), lambda bi, ci: (bi, ci, 0))
    ssm_spec = pl.BlockSpec((1, SSD_HEADS, SSD_HEAD_DIM, SSD_STATE), lambda bi, ci: (bi, 0, 0, 0))
    return pl.pallas_call(
        _ssd_kernel,
        grid=(b, t // cl),
        in_specs=[
            tok(CONV_DIM), tok(D_INNER), tok(DT_PAD),
            pl.BlockSpec((1, SUBLANES, CONV_DIM), lambda bi, ci: (bi, 0, 0)),
            ssm_spec,
            _const_spec((CONV_WIDTH, CONV_DIM)), _const_spec((1, CONV_DIM)),
            _const_spec((1, DT_PAD)), _const_spec((1, DT_PAD)),
            _const_spec((1, D_INNER)), _const_spec((1, D_INNER)),
        ],
        out_specs=[tok(D_INNER), ssm_spec],
        out_shape=[jax.ShapeDtypeStruct((b, t, D_INNER), F32),
                   jax.ShapeDtypeStruct((b, SSD_HEADS, SSD_HEAD_DIM, SSD_STATE), F32)],
        scratch_shapes=[pltpu.VMEM((SUBLANES, CONV_DIM), F32),
                        pltpu.VMEM((SSD_GROUPS, SSD_STATE, _STATE_W), F32),
                        pltpu.VMEM((cl, D_INNER), F32)],
        compiler_params=pltpu.CompilerParams(dimension_semantics=("parallel", "arbitrary")),
        name="ssd",
    )(xbc, z, dt, conv0, ssm0, cw, cb, dtb, alog, dsk, nrm)


_SS_BB = 8
_HEAD_ROWS = 2 * SUBLANES


def _heads_to_features(a):
    return jnp.concatenate([jnp.broadcast_to(a[h:h + 1], (SSD_HEAD_DIM, LANES)) for h in range(SSD_HEADS)], axis=0)


def _ssd_sample_kernel(n_tok, zxt_ref, st_ref, dtt_ref, ssm0_ref, ccol_ref, hcol_ref, ycol_ref, y_ref, ssm_ref):
    group = n_tok * _SS_BB
    sub = lax.rem(pl.program_id(0), LANES // group)
    shift = lax.rem(LANES - group * sub, LANES)
    roll = lambda a, s: pltpu.roll(a, s, axis=1)
    lane = lax.broadcasted_iota(jnp.int32, (1, LANES), 1)
    tok = lane % n_tok
    seq = lane // n_tok

    zx = roll(zxt_ref[...], shift)
    zt, xt = zx[:D_INNER], zx[D_INNER:]
    stt = roll(st_ref[...].T, shift)
    dtr = roll(dtt_ref[...], shift)

    xc = ccol_ref[:, CONV_WIDTH:CONV_WIDTH + 1] + ccol_ref[:, CONV_WIDTH - 1:CONV_WIDTH] * xt
    for k in range(CONV_WIDTH - 1):
        back = CONV_WIDTH - 1 - k
        src = jnp.where(tok >= back, roll(xt, back), roll(stt, (LANES - k) % LANES))
        xc = xc + ccol_ref[:, k:k + 1] * src
    xc = xc * _sigmoid(xc)
    gn = SSD_GROUPS * SSD_STATE
    xs = xc[:D_INNER]
    bmb = xc[D_INNER:D_INNER + gn].astype(BF16)
    cmb = xc[D_INNER + gn:].astype(BF16)
    bmf, cmf = bmb.astype(F32), cmb.astype(F32)

    dt = _softplus(dtr + hcol_ref[:, 0:1])
    da = dt * (-jnp.exp(hcol_ref[:, 1:2]))
    acum = da
    for s in range(1, n_tok):
        acum = acum + jnp.where(tok >= s, roll(da, s), 0.0)
    last = jnp.where(tok == n_tok - 1, acum, 0.0)
    tot = last
    for s in range(1, n_tok):
        tot = tot + roll(last, LANES - s)
    xw = xs * _heads_to_features(jnp.exp(tot - acum) * dt)
    eac_e = _heads_to_features(jnp.exp(acum))
    cd_e = _heads_to_features(jnp.exp(tot))

    row = lax.broadcasted_iota(jnp.int32, (_HEAD_ROWS, LANES), 0)
    ydiag = None
    for d in range(n_tok):
        prod = cmf * (bmf if d == 0 else roll(bmf, d))
        cb = jnp.zeros((_HEAD_ROWS, LANES), F32)
        for g in range(SSD_GROUPS):
            cbg = jnp.sum(prod[g * SSD_STATE:(g + 1) * SSD_STATE], axis=0, keepdims=True)
            in_group = (row >= SSD_HEADS_PER_GROUP * g) & (row < SSD_HEADS_PER_GROUP * (g + 1))
            cb = jnp.where(in_group, cbg, cb)
        if d == 0:
            coef, xsh = cb * dt, xs
        else:
            ok = tok >= d
            decay = jnp.exp(jnp.where(ok, acum - roll(acum, d), 0.0))
            coef, xsh = jnp.where(ok, cb * decay * roll(dt, d), 0.0), roll(xs, d)
        term = _heads_to_features(coef) * xsh
        ydiag = term if ydiag is None else ydiag + term

    yoffs = []
    for g in range(SSD_GROUPS):
        hs = slice(SSD_HEADS_PER_GROUP * g, SSD_HEADS_PER_GROUP * (g + 1))
        rs = slice(_GROUP_W * g, _GROUP_W * (g + 1))
        cm_g, bm_g = cmb[g * SSD_STATE:(g + 1) * SSD_STATE], bmb[g * SSD_STATE:(g + 1) * SSD_STATE]
        xw_g, cd_g = xw[rs], cd_e[rs]
        h0s = [ssm0_ref[bb, hs].reshape(_GROUP_W, SSD_STATE) for bb in range(_SS_BB)]
        offs = [jnp.dot(h0.astype(BF16), cm_g, preferred_element_type=F32) for h0 in h0s]
        adds = [lax.dot_general(jnp.where(seq == bb, xw_g, 0.0).astype(BF16), bm_g, _NT, preferred_element_type=F32)
                for bb in range(_SS_BB)]
        yo = jnp.zeros((_GROUP_W, LANES), F32)
        for bb in range(_SS_BB):
            yo = jnp.where(seq == bb, offs[bb], yo)
            new = cd_g[:, n_tok * bb:n_tok * bb + 1] * h0s[bb] + adds[bb]
            ssm_ref[bb, hs] = new.reshape(SSD_HEADS_PER_GROUP, SSD_HEAD_DIM, SSD_STATE)
        yoffs.append(yo)

    y = ydiag + jnp.concatenate(yoffs, axis=0) * eac_e + ycol_ref[:, 0:1] * xs
    y = y * (zt * _sigmoid(zt))
    y = y * lax.rsqrt(jnp.mean(y * y, axis=0, keepdims=True) + RMS_EPS) * ycol_ref[:, 1:2]
    y_ref[...] = y.T[0:group, :]


def _ssd_sample(zxt, st_tok, dtt, ssm0, ccol, hcol, ycol, n_tok):
    b = ssm0.shape[0]
    m = b * n_tok
    group = n_tok * _SS_BB
    per_tile = LANES // group
    ssm_spec = pl.BlockSpec((_SS_BB, SSD_HEADS, SSD_HEAD_DIM, SSD_STATE), lambda i: (i, 0, 0, 0))
    return pl.pallas_call(
        functools.partial(_ssd_sample_kernel, n_tok),
        grid=(b // _SS_BB,),
        in_specs=[
            pl.BlockSpec((D_INNER + CONV_DIM, LANES), lambda i: (0, i // per_tile)),
            pl.BlockSpec((LANES, CONV_DIM), lambda i: (i // per_tile, 0)),
            pl.BlockSpec((_HEAD_ROWS, LANES), lambda i: (0, i // per_tile)),
            ssm_spec,
            _const_spec(ccol.shape), _const_spec(hcol.shape), _const_spec(ycol.shape),
        ],
        out_specs=[pl.BlockSpec((group, D_INNER), lambda i: (i, 0)), ssm_spec],
        out_shape=[jax.ShapeDtypeStruct((m, D_INNER), F32), jax.ShapeDtypeStruct(ssm0.shape, F32)],
        compiler_params=pltpu.CompilerParams(dimension_semantics=("parallel",)),
        name="ssd_sample",
    )(zxt, st_tok, dtt, ssm0, ccol, hcol, ycol)


def _load_tokens(ref, scr, dil):
    if dil == 1:
        return ref[0]
    n = ref.shape[1]
    halves = range(D_BRANCH // LANES)
    for r in range(dil):
        for half in halves:
            c0 = r * D_BRANCH + half * LANES
            scr[half, pl.ds(r, n, stride=dil), :] = ref[0, :, c0:c0 + LANES]
    return jnp.concatenate([scr[half] for half in halves], axis=1)


def _ffn_tail(h, gffn_ref, wg_ref, wu_ref, wd_ref, gfin_ref, y_ref, act_ref, final):
    hn = _rms(h, gffn_ref[...]).astype(BF16)
    for c in range(0, D_FF, FF_CHUNK):
        gate = jnp.dot(hn, wg_ref[:, c:c + FF_CHUNK], preferred_element_type=F32)
        up = jnp.dot(hn, wu_ref[:, c:c + FF_CHUNK], preferred_element_type=F32)
        act_ref[:, c:c + FF_CHUNK] = (gate * _sigmoid(gate) * up).astype(BF16)
    y = h + jnp.dot(act_ref[...], wd_ref[...], preferred_element_type=F32)
    y_ref[...] = _rms(y, gfin_ref[...]) if final else y


def _out_ffn_prompt_kernel(final, x_ref, *refs):
    att_refs = refs[:2 * N_BRANCH]
    ssd_ref, wo_ref, gffn_ref, wg_ref, wu_ref, wd_ref, gfin_ref, y_ref, act_ref = refs[2 * N_BRANCH:-2 * N_BRANCH]
    scrs = refs[-2 * N_BRANCH:]
    os_, ls = [], []
    for g, (win, dil) in enumerate(ATT_BRANCHES):
        os_.append(_load_tokens(att_refs[2 * g], scrs[2 * g], dil))
        ls.append(_load_tokens(att_refs[2 * g + 1], scrs[2 * g + 1], dil))
    mx = jnp.maximum(jnp.maximum(ls[0], ls[1]), ls[2])
    es = [jnp.exp(l - mx) for l in ls]
    inv = 1.0 / (es[0] + es[1] + es[2])
    h = x_ref[...]
    for g in range(N_BRANCH):
        a = (os_[g] * (es[g] * inv)).astype(BF16)
        h = h + jnp.dot(a, wo_ref[g * D_BRANCH:(g + 1) * D_BRANCH, :], preferred_element_type=F32)
    h = h + jnp.dot(ssd_ref[...].astype(BF16), wo_ref[D_ATT:D_MIX, :], preferred_element_type=F32)
    _ffn_tail(h, gffn_ref, wg_ref, wu_ref, wd_ref, gfin_ref, y_ref, act_ref, final)


def _out_ffn_sample_kernel(final, x_ref, att_ref, ssd_ref, wo_ref, gffn_ref, wg_ref, wu_ref, wd_ref, gfin_ref,
                           y_ref, act_ref):
    h = x_ref[...]
    h = h + jnp.dot(att_ref[...].astype(BF16), wo_ref[0:D_ATT, :], preferred_element_type=F32)
    h = h + jnp.dot(ssd_ref[...].astype(BF16), wo_ref[D_ATT:D_MIX, :], preferred_element_type=F32)
    _ffn_tail(h, gffn_ref, wg_ref, wu_ref, wd_ref, gfin_ref, y_ref, act_ref, final)


def _ffn_weight_specs():
    return [_const_spec((D_MIX, D_MODEL)), _const_spec((1, D_MODEL)),
            _const_spec((D_MODEL, D_FF)), _const_spec((D_MODEL, D_FF)),
            _const_spec((D_FF, D_MODEL)), _const_spec((1, D_MODEL))]


def _out_ffn_prompt(x2d, b, t, att, ssd2d, weights, final):
    m = b * t
    tm = FFN_TM
    tpb = t // tm
    tok = lambda w: pl.BlockSpec((tm, w), lambda i: (i, 0))
    att_specs = []
    for win, dil in ATT_BRANCHES:
        att_specs += [pl.BlockSpec((1, tm // dil, dil * D_BRANCH), lambda i: (i // tpb, i % tpb, 0))] * 2
    return pl.pallas_call(
        functools.partial(_out_ffn_prompt_kernel, final),
        grid=(m // tm,),
        in_specs=[tok(D_MODEL)] + att_specs + [tok(D_INNER)] + _ffn_weight_specs(),
        out_specs=tok(D_MODEL),
        out_shape=jax.ShapeDtypeStruct((m, D_MODEL), F32),
        scratch_shapes=[pltpu.VMEM((tm, D_FF), BF16)]
        + [pltpu.VMEM((D_BRANCH // LANES, tm, LANES), F32)] * (2 * N_BRANCH),
        compiler_params=pltpu.CompilerParams(dimension_semantics=("parallel",)),
        name="out_ffn_prompt",
    )(x2d, *att, ssd2d, *weights)


def _out_ffn_sample(x2d, att2d, ssd2d, weights, final):
    m = x2d.shape[0]
    tm = m // 2
    tok = lambda w: pl.BlockSpec((tm, w), lambda i: (i, 0))
    return pl.pallas_call(
        functools.partial(_out_ffn_sample_kernel, final),
        grid=(m // tm,),
        in_specs=[tok(D_MODEL), tok(D_ATT), tok(D_INNER)] + _ffn_weight_specs(),
        out_specs=tok(D_MODEL),
        out_shape=jax.ShapeDtypeStruct((m, D_MODEL), F32),
        scratch_shapes=[pltpu.VMEM((tm, D_FF), BF16)],
        compiler_params=pltpu.CompilerParams(dimension_semantics=("parallel",)),
        name="out_ffn_sample",
    )(x2d, att2d, ssd2d, *weights)


def _pad_lanes(v, width):
    return jnp.pad(v, ((0, 0), (0, width - v.shape[1])))


def kernel(x_prompt, x_sample, cache_kv_d1, cache_kv_d4, cache_kv_d16, state_conv, state_ssm, norm_mix, w_in, conv_w, conv_b, dt_bias, a_log, d_skip, norm_ssd, w_out, norm_ffn, w_gate_up, w_down, norm_final):
    bp, tp, _ = x_prompt.shape
    bs, ts, _ = x_sample.shape
    depth = w_in.shape[0]
    caches = (cache_kv_d1, cache_kv_d4, cache_kv_d16)
    hp, hs = x_prompt.reshape(bp * tp, D_MODEL), x_sample.reshape(bs * ts, D_MODEL)
    gfin = norm_final.reshape(1, D_MODEL)
    p_new, s_new = [[] for _ in range(5)], [[] for _ in range(5)]

    for l in range(depth):
        wi = w_in[l]
        wk, wv = wi[:, D_ATT:2 * D_ATT], wi[:, 2 * D_ATT:3 * D_ATT]
        wq = wi[:, :D_ATT].astype(BF16)
        wkv = jnp.concatenate([w[:, g * D_BRANCH:(g + 1) * D_BRANCH] for g in range(N_BRANCH) for w in (wk, wv)],
                              axis=1).astype(BF16)
        o = 3 * D_ATT
        wzx = wi[:, o:o + D_INNER + CONV_DIM].astype(BF16)
        wdt = _pad_lanes(wi[:, o + D_INNER + CONV_DIM:], DT_PAD).astype(BF16)
        gmix = norm_mix[l].reshape(1, D_MODEL)
        ssd_params = (conv_w[l], conv_b[l].reshape(1, CONV_DIM),
                      _pad_lanes(dt_bias[l].reshape(1, SSD_HEADS), DT_PAD),
                      _pad_lanes(a_log[l].reshape(1, SSD_HEADS), DT_PAD),
                      jnp.repeat(d_skip[l], SSD_HEAD_DIM).reshape(1, D_INNER),
                      norm_ssd[l].reshape(1, D_INNER))
        ffn_w = (w_out[l].astype(BF16), norm_ffn[l].reshape(1, D_MODEL),
                 w_gate_up[l][:, :D_FF].astype(BF16), w_gate_up[l][:, D_FF:].astype(BF16),
                 w_down[l].astype(BF16), gfin)
        final = l == depth - 1

        qkv, kts, (z, xbc, dt) = _in_proj_prompt(hp, bp, tp, gmix, wq, wkv, wzx, wdt)
        for g, (win, dil) in enumerate(ATT_BRANCHES):
            kt = kts[g].reshape(bp, 2, HEADS_PER_BRANCH, HEAD_DIM, win)
            p_new[g].append(jnp.transpose(kt, (0, 4, 1, 2, 3)))
        att = _attn_prompt(qkv, bp, tp)
        xbc3 = xbc.reshape(bp, tp, CONV_DIM)
        conv0 = jnp.zeros((bp, SUBLANES, CONV_DIM), F32)
        ssm0 = jnp.zeros((bp, SSD_HEADS, SSD_HEAD_DIM, SSD_STATE), F32)
        y_ssd, ssm_p = _ssd(xbc3, z.reshape(bp, tp, D_INNER), dt.reshape(bp, tp, DT_PAD), conv0, ssm0, ssd_params)
        p_new[3].append(xbc3[:, tp - (CONV_WIDTH - 1):])
        p_new[4].append(ssm_p)
        hp = _out_ffn_prompt(hp, bp, tp, att, y_ssd.reshape(bp * tp, D_INNER), ffn_w, final)

        assert ts >= CONV_WIDTH - 1 and LANES % (ts * _SS_BB) == 0 and bs % _SS_BB == 0
        q, xbc, new_t, zxt, dtt = _in_proj_sample(hs, gmix, wq, wzx[:, D_INNER:], wkv.T, wzx.T, wdt.T)
        caches_t = [jnp.transpose(c[l], (0, 2, 3, 4, 1)).reshape(bs, KV_W, c.shape[2]) for c in caches]
        att_s, shifted = _sample_attn(q.reshape(bs, ts, D_ATT), new_t, caches_t)
        for g in range(N_BRANCH):
            lg = shifted[g].shape[2]
            s_new[g].append(jnp.transpose(shifted[g].reshape(bs, 2, HEADS_PER_BRANCH, HEAD_DIM, lg), (0, 4, 1, 2, 3)))
        xbc3 = xbc.reshape(bs, ts, CONV_DIM)
        st_tok = jnp.pad(state_conv[l], ((0, 0), (0, ts - (CONV_WIDTH - 1)), (0, 0))).reshape(bs * ts, CONV_DIM)
        pad_heads = lambda v: jnp.pad(v, (0, _HEAD_ROWS - SSD_HEADS))
        ccol = jnp.concatenate([conv_w[l].T, conv_b[l][:, None]], axis=1)
        hcol = jnp.stack([pad_heads(dt_bias[l]), pad_heads(a_log[l])], axis=1)
        ycol = jnp.stack([jnp.repeat(d_skip[l], SSD_HEAD_DIM), norm_ssd[l]], axis=1)
        y_ssd, ssm_s = _ssd_sample(zxt, st_tok, dtt, state_ssm[l], ccol, hcol, ycol, ts)
        s_new[3].append(jnp.concatenate([state_conv[l], xbc3], axis=1)[:, -(CONV_WIDTH - 1):])
        s_new[4].append(ssm_s)
        hs = _out_ffn_sample(hs, att_s.reshape(bs * ts, D_ATT), y_ssd, ffn_w, final)

    y_prompt = hp.reshape(bp, tp, D_MODEL)
    y_sample = hs.reshape(bs, ts, D_MODEL)
    p_out = [jnp.stack(a) for a in p_new]
    s_out = [jnp.stack(a) for a in s_new]
    return (y_prompt, y_sample, *p_out, *s_out)
```

```python
import functools

import jax
import jax.numpy as jnp
from jax import lax
from jax.experimental import pallas as pl
from jax.experimental.pallas import tpu as pltpu

F32 = jnp.float32
BF16 = jnp.bfloat16

D_MODEL = 1024
HEAD_DIM = 64
ATT_BRANCHES = ((128, 1), (512, 4), (2048, 16))
N_BRANCH = 3
HEADS_PER_BRANCH = 4
D_BRANCH = HEADS_PER_BRANCH * HEAD_DIM
D_ATT = N_BRANCH * D_BRANCH
SSD_HEADS = 12
SSD_HEAD_DIM = 64
D_INNER = SSD_HEADS * SSD_HEAD_DIM
SSD_GROUPS = 4
SSD_HEADS_PER_GROUP = 3
SSD_STATE = 128
CONV_WIDTH = 4
CONV_DIM = D_INNER + 2 * SSD_GROUPS * SSD_STATE
SSD_CHUNK = 128
D_MIX = D_ATT + D_INNER
D_FF = 2816
RMS_EPS = 1e-5
NEG_INF = -1e30
ATT_SCALE = HEAD_DIM ** -0.5
BAND = 128

LANES = 128
SUBLANES = 8
KV_W = 2 * D_BRANCH
DT_PAD = LANES
FF_CHUNK = 256
PROJ_CHUNK = 256
PROJ_TM = 512
ATT_QB = 2 * BAND
FFN_TM = 512

_NT = (((1,), (1,)), ((), ()))


def _rms(x, g):
    inv = lax.rsqrt(jnp.mean(x * x, axis=-1, keepdims=True) + RMS_EPS)
    return x * inv * g


def _sigmoid(x):
    return 1.0 / (1.0 + jnp.exp(-x))


def _softplus(x):
    return jnp.maximum(x, 0.0) + jnp.log1p(jnp.exp(-jnp.abs(x)))


def _const_spec(shape):
    nd = len(shape)
    return pl.BlockSpec(shape, lambda *_: (0,) * nd, pipeline_mode=pl.Buffered(1))


def _proj_store(xn, w_ref, out_ref, width):
    for c in range(0, width, PROJ_CHUNK):
        w = min(PROJ_CHUNK, width - c)
        out_ref[:, c:c + w] = jnp.dot(xn, w_ref[:, c:c + w], preferred_element_type=F32)


_Q_TILES = D_ATT // LANES
_KV_TILES = KV_W // LANES
_HALVES = D_BRANCH // LANES


def _in_proj_prompt_kernel(tiles_per_batch, x_ref, g_ref, wq_ref, wkv_ref, wzx_ref, wdt_ref, *refs):
    qkv_refs = refs[:3 * N_BRANCH]
    kt_refs = refs[3 * N_BRANCH:4 * N_BRANCH]
    z_ref, xbc_ref, dt_ref, acc = refs[4 * N_BRANCH:]
    tm = x_ref.shape[0]
    ti = lax.rem(pl.program_id(0), tiles_per_batch)
    xn = _rms(x_ref[...], g_ref[...]).astype(BF16)

    for c in range(0, D_ATT, PROJ_CHUNK):
        res = jnp.dot(xn, wq_ref[:, c:c + PROJ_CHUNK], preferred_element_type=F32) * ATT_SCALE
        for j in range(PROJ_CHUNK // LANES):
            acc[c // LANES + j] = res[:, j * LANES:(j + 1) * LANES]
    for c in range(0, N_BRANCH * KV_W, PROJ_CHUNK):
        res = jnp.dot(xn, wkv_ref[:, c:c + PROJ_CHUNK], preferred_element_type=F32)
        for j in range(PROJ_CHUNK // LANES):
            acc[_Q_TILES + c // LANES + j] = res[:, j * LANES:(j + 1) * LANES]
    _proj_store(xn, wzx_ref.at[:, 0:D_INNER], z_ref, D_INNER)
    _proj_store(xn, wzx_ref.at[:, D_INNER:D_INNER + CONV_DIM], xbc_ref, CONV_DIM)
    _proj_store(xn, wdt_ref, dt_ref, DT_PAD)

    for g, (win, dil) in enumerate(ATT_BRANCHES):
        n = tm // dil
        tiles = (_HALVES * g, _Q_TILES + _KV_TILES * g, _Q_TILES + _KV_TILES * g + _HALVES)
        for out_ref, t0 in zip(qkv_refs[3 * g:3 * g + 3], tiles):
            for r in range(dil):
                for half in range(_HALVES):
                    c0 = r * D_BRANCH + half * LANES
                    rows = acc[t0 + half] if dil == 1 else acc[t0 + half, pl.ds(r, n, stride=dil), :]
                    out_ref[0, :, c0:c0 + LANES] = rows.astype(BF16)

    for g, (win, dil) in enumerate(ATT_BRANCHES):
        w = kt_refs[g].shape[2]
        first = tiles_per_batch - win // w

        @pl.when(ti >= first)
        def _(g=g, w=w):
            for j in range(_KV_TILES):
                kt_refs[g][0, j * LANES:(j + 1) * LANES, :] = acc[_Q_TILES + _KV_TILES * g + j, tm - w:tm, :].T


def _in_proj_prompt(x2d, b, t, g, wq, wkv, wzx, wdt):
    m = b * t
    tm = PROJ_TM
    tpb = t // tm
    tok = lambda w: pl.BlockSpec((tm, w), lambda i: (i, 0))
    qkv_specs, qkv_shapes, kt_specs, kt_shapes = [], [], [], []
    for win, dil in ATT_BRANCHES:
        spec = pl.BlockSpec((1, tm // dil, dil * D_BRANCH), lambda i: (i // tpb, i % tpb, 0))
        shape = jax.ShapeDtypeStruct((b, t // dil, dil * D_BRANCH), BF16)
        qkv_specs += [spec] * 3
        qkv_shapes += [shape] * 3
        w = min(win, tm)
        first = tpb - win // w
        kt_specs.append(pl.BlockSpec((1, KV_W, w), lambda i, first=first: (i // tpb, 0, jnp.maximum(i % tpb - first, 0))))
        kt_shapes.append(jax.ShapeDtypeStruct((b, KV_W, win), F32))
    outs = pl.pallas_call(
        functools.partial(_in_proj_prompt_kernel, tpb),
        grid=(m // tm,),
        in_specs=[tok(D_MODEL), _const_spec((1, D_MODEL)), _const_spec((D_MODEL, D_ATT)),
                  _const_spec((D_MODEL, N_BRANCH * KV_W)), _const_spec((D_MODEL, D_INNER + CONV_DIM)),
                  _const_spec((D_MODEL, DT_PAD))],
        out_specs=qkv_specs + kt_specs + [tok(D_INNER), tok(CONV_DIM), tok(DT_PAD)],
        out_shape=qkv_shapes + kt_shapes + [jax.ShapeDtypeStruct((m, w), F32) for w in (D_INNER, CONV_DIM, DT_PAD)],
        scratch_shapes=[pltpu.VMEM((_Q_TILES + N_BRANCH * _KV_TILES, tm, LANES), F32)],
        compiler_params=pltpu.CompilerParams(dimension_semantics=("arbitrary",)),
        name="in_proj_prompt",
    )(x2d, g, wq, wkv, wzx, wdt)
    n = 3 * N_BRANCH
    return outs[:n], outs[n:n + N_BRANCH], outs[n + N_BRANCH:]


def _in_proj_sample_kernel(x_ref, g_ref, wq_ref, wxbc_ref, wkvt_ref, wzxt_ref, wdtt_ref,
                           q_ref, xbc_ref, kvt_ref, zxt_ref, dtt_ref):
    xn = _rms(x_ref[...], g_ref[...]).astype(BF16)
    for c in range(0, D_ATT, PROJ_CHUNK):
        q = jnp.dot(xn, wq_ref[:, c:c + PROJ_CHUNK], preferred_element_type=F32) * ATT_SCALE
        q_ref[:, c:c + PROJ_CHUNK] = q.astype(BF16).astype(F32)
    _proj_store(xn, wxbc_ref, xbc_ref, CONV_DIM)
    for wt_ref, out_ref in ((wkvt_ref, kvt_ref), (wzxt_ref, zxt_ref), (wdtt_ref, dtt_ref)):
        for r in range(0, wt_ref.shape[0], PROJ_CHUNK):
            n = min(PROJ_CHUNK, wt_ref.shape[0] - r)
            out_ref[r:r + n, :] = lax.dot_general(wt_ref[r:r + n, :], xn, _NT, preferred_element_type=F32)


def _in_proj_sample(x2d, g, wq, wxbc, wkvt, wzxt, wdtt):
    m = x2d.shape[0]
    ins = (x2d, g, wq, wxbc, wkvt, wzxt, wdtt)
    shapes = [(m, D_ATT), (m, CONV_DIM), (wkvt.shape[0], m), (wzxt.shape[0], m), (wdtt.shape[0], m)]
    return pl.pallas_call(
        _in_proj_sample_kernel,
        grid=(1,),
        in_specs=[_const_spec(a.shape) for a in ins],
        out_specs=[pl.BlockSpec(s, lambda i: (0, 0)) for s in shapes],
        out_shape=[jax.ShapeDtypeStruct(s, F32) for s in shapes],
        name="in_proj_sample",
    )(*ins)


_ATT_STEPS = 16
_ATT_ROWS = 128


def _attn_prompt_kernel(*refs):
    ins, outs = refs[:5 * N_BRANCH], refs[5 * N_BRANCH:]
    j = pl.program_id(1)
    qi = lax.broadcasted_iota(jnp.int32, (BAND, 2 * BAND), 0)
    kj = lax.broadcasted_iota(jnp.int32, (BAND, 2 * BAND), 1)
    dist = BAND + qi - kj
    in_band = (dist >= 0) & (dist <= BAND)

    tasks = []
    for g, (win, dil) in enumerate(ATT_BRANCHES):
        q_ref, kp_ref, kc_ref, vp_ref, vc_ref = ins[5 * g:5 * g + 5]
        i = lax.rem(j, _ATT_STEPS // dil)
        kc, vc = kc_ref[0], vc_ref[0]
        for sb in range(ATT_QB // BAND):
            rows = slice(sb * BAND, (sb + 1) * BAND)
            if sb == 0:
                k = jnp.concatenate([kp_ref[0], kc[rows]], axis=0)
                v = jnp.concatenate([vp_ref[0], vc[rows]], axis=0)
                valid = in_band & ((kj >= BAND) | (i > 0))
            else:
                k = kc[(sb - 1) * BAND:(sb + 1) * BAND]
                v = vc[(sb - 1) * BAND:(sb + 1) * BAND]
                valid = in_band
            for h in range(HEADS_PER_BRANCH):
                sl = slice(h * HEAD_DIM, (h + 1) * HEAD_DIM)
                for r0 in range(0, BAND, _ATT_ROWS):
                    sub = slice(sb * BAND + r0, sb * BAND + r0 + _ATT_ROWS)
                    tasks.append((q_ref[0, sub, sl], k[:, sl], v[:, sl], valid[r0:r0 + _ATT_ROWS],
                                  outs[2 * g], outs[2 * g + 1], sub, sl))

    ss = [jnp.where(t[3], lax.dot_general(t[0], t[1], _NT, preferred_element_type=F32), NEG_INF) for t in tasks]
    ms = [jnp.max(s, axis=-1, keepdims=True) for s in ss]
    ps = [jnp.exp(s - m) for s, m in zip(ss, ms)]
    dens = [jnp.sum(p, axis=-1, keepdims=True) for p in ps]
    nums = [jnp.dot(p.astype(BF16), t[2], preferred_element_type=F32) for p, t in zip(ps, tasks)]
    for t, m, den, num in zip(tasks, ms, dens, nums):
        o_ref, l_ref, rows, sl = t[4:]
        o_ref[0, rows, sl] = num / den
        l_ref[0, rows, sl] = jnp.broadcast_to(m + jnp.log(den), (_ATT_ROWS, HEAD_DIM))


def _attn_prompt(qkv, b, t):
    assert t // ATT_QB == _ATT_STEPS
    in_specs, args, out_specs, out_shapes = [], [], [], []
    for g, (win, dil) in enumerate(ATT_BRANCHES):
        nblk = _ATT_STEPS // dil
        cur = pl.BlockSpec((1, ATT_QB, D_BRANCH), lambda bi, j, nblk=nblk: (bi, j % nblk, j // nblk))
        prv = pl.BlockSpec((1, BAND, D_BRANCH),
                           lambda bi, j, nblk=nblk: (bi, jnp.maximum((ATT_QB // BAND) * (j % nblk) - 1, 0), j // nblk))
        q, k, v = qkv[3 * g:3 * g + 3]
        in_specs += [cur, prv, cur, prv, cur]
        args += [q, k, k, v, v]
        out_specs += [cur, cur]
        out_shapes += [jax.ShapeDtypeStruct((b, t // dil, dil * D_BRANCH), F32)] * 2
    return pl.pallas_call(
        _attn_prompt_kernel,
        grid=(b, _ATT_STEPS),
        in_specs=in_specs,
        out_specs=out_specs,
        out_shape=out_shapes,
        compiler_params=pltpu.CompilerParams(dimension_semantics=("parallel", "arbitrary")),
        name="attn_prompt",
    )(*args)


_QROWS = 4 * SUBLANES
_SHIFT_ROWS = 64


def _sample_attn_body(n_new, b, q_ref, nt_ref, c1_ref, c4_ref, c16_ref,
                      att_ref, o1_ref, o4_ref, o16_ref, nbuf):
    per_tile = LANES // n_new
    new0 = LANES - n_new
    shift = new0 - n_new * lax.rem(b, per_tile)
    nbuf[...] = pltpu.roll(nt_ref[...], shift, axis=1)

    q = q_ref[0]
    row = lax.broadcasted_iota(jnp.int32, (_QROWS, D_BRANCH), 0)
    lane = lax.broadcasted_iota(jnp.int32, (_QROWS, D_BRANCH), 1)
    hmask = (lane // HEAD_DIM) == (row % SUBLANES)

    us, dens, lses = [], [], []
    for g, ((win, dil), c_ref) in enumerate(zip(ATT_BRANCHES, (c1_ref, c4_ref, c16_ref))):
        length = c_ref.shape[2]
        qs = q[:, g * D_BRANCH:(g + 1) * D_BRANCH]
        qb = jnp.concatenate([jnp.broadcast_to(qs[s:s + 1], (SUBLANES, D_BRANCH)) for s in range(n_new)], axis=0)
        qbd = jnp.where(hmask, qb, 0.0).astype(BF16)
        kt = c_ref[0, 0:D_BRANCH, :].astype(BF16)
        vt = c_ref[0, D_BRANCH:KV_W, :].astype(BF16)
        kn = nbuf[g * KV_W:g * KV_W + D_BRANCH, :].astype(BF16)
        vn = nbuf[g * KV_W + D_BRANCH:(g + 1) * KV_W, :].astype(BF16)
        sc = jnp.dot(qbd, kt, preferred_element_type=F32)
        scn = jnp.dot(qbd, kn, preferred_element_type=F32)
        s_c = lax.broadcasted_iota(jnp.int32, (_QROWS, length), 0) // SUBLANES
        l_c = lax.broadcasted_iota(jnp.int32, (_QROWS, length), 1)
        valid_c = (l_c >= s_c) & (((l_c - s_c) & (dil - 1)) == 0)
        s_n = lax.broadcasted_iota(jnp.int32, (_QROWS, LANES), 0) // SUBLANES
        j_n = lax.broadcasted_iota(jnp.int32, (_QROWS, LANES), 1) - new0
        valid_n = (j_n >= 0) & (j_n <= s_n) & (((s_n - j_n) & (dil - 1)) == 0)
        sc = jnp.where(valid_c, sc, NEG_INF)
        scn = jnp.where(valid_n, scn, NEG_INF)
        m = jnp.maximum(jnp.max(sc, axis=-1, keepdims=True), jnp.max(scn, axis=-1, keepdims=True))
        p = jnp.exp(sc - m)
        pn = jnp.exp(scn - m)
        den = jnp.sum(p, axis=-1, keepdims=True) + jnp.sum(pn, axis=-1, keepdims=True)
        u = (lax.dot_general(p.astype(BF16), vt, _NT, preferred_element_type=F32)
             + lax.dot_general(pn.astype(BF16), vn, _NT, preferred_element_type=F32))
        us.append(u)
        dens.append(den)
        lses.append(m + jnp.log(den))

    mx = jnp.maximum(jnp.maximum(lses[0], lses[1]), lses[2])
    es = [jnp.exp(l - mx) for l in lses]
    tot = es[0] + es[1] + es[2]
    for g in range(N_BRANCH):
        coef = es[g] / tot / dens[g]
        a = jnp.where(hmask, us[g] * coef, 0.0)
        a = a.reshape(n_new, SUBLANES, D_BRANCH).sum(axis=1)
        att_ref[0, :, g * D_BRANCH:(g + 1) * D_BRANCH] = a

    def shift_slab(g, c_ref, o_ref, r0):
        keep = lax.broadcasted_iota(jnp.int32, (_SHIFT_ROWS, LANES), 1) < new0
        ntiles = c_ref.shape[2] // LANES
        rows = slice(r0, r0 + _SHIFT_ROWS)
        nxt = pltpu.roll(c_ref[0, rows, 0:LANES], new0, axis=1)
        for j in range(ntiles):
            cur = nxt
            if j + 1 < ntiles:
                nxt = pltpu.roll(c_ref[0, rows, (j + 1) * LANES:(j + 2) * LANES], new0, axis=1)
            else:
                nxt = nbuf[g * KV_W + r0:g * KV_W + r0 + _SHIFT_ROWS, :]
            o_ref[0, rows, j * LANES:(j + 1) * LANES] = jnp.where(keep, cur, nxt)

    return [functools.partial(shift_slab, g, c_ref, o_ref, r0)
            for r0 in range(0, KV_W, _SHIFT_ROWS)
            for g, (c_ref, o_ref) in enumerate(zip((c1_ref, c4_ref, c16_ref), (o1_ref, o4_ref, o16_ref)))]


_GROUP_W = SSD_HEADS_PER_GROUP * SSD_HEAD_DIM
_STATE_W = 2 * LANES


def _split3(a):
    a1 = a.astype(BF16)
    r1 = a - a1.astype(F32)
    a2 = r1.astype(BF16)
    a3 = (r1 - a2.astype(F32)).astype(BF16)
    return jnp.concatenate([a1, a2, a3], axis=1)


def _ssd_kernel(xbc_ref, z_ref, dt_ref, conv0_ref, ssm0_ref, cw_ref, cb_ref,
                dtb_ref, alog_ref, dsk_ref, nrm_ref, y_ref, ssm_ref, tail, st, ybuf):
    c = pl.program_id(1)
    nc = pl.num_programs(1)
    cl = SSD_CHUNK

    @pl.when(c == 0)
    def _():
        tail[...] = conv0_ref[0]
        zpad = jnp.zeros((_STATE_W - _GROUP_W, SSD_STATE), F32)
        for g in range(SSD_GROUPS):
            s = jnp.concatenate([ssm0_ref[0, SSD_HEADS_PER_GROUP * g + r] for r in range(SSD_HEADS_PER_GROUP)]
                                + [zpad], axis=0)
            st[g] = s.T

    xcur = xbc_ref[0]
    prev = tail[...]
    row8 = lax.broadcasted_iota(jnp.int32, (SUBLANES, CONV_DIM), 0)
    xc = cb_ref[...] + cw_ref[CONV_WIDTH - 1:CONV_WIDTH, :] * xcur
    for s in range(1, CONV_WIDTH):
        sh = pltpu.roll(xcur, s, axis=0)
        head = jnp.where(row8 < s, pltpu.roll(prev, s, axis=0), sh[0:SUBLANES])
        sh = jnp.concatenate([head, sh[SUBLANES:]], axis=0)
        xc = xc + cw_ref[CONV_WIDTH - 1 - s:CONV_WIDTH - s, :] * sh
    tail[...] = xcur[cl - SUBLANES:cl]
    xc = xc * _sigmoid(xc)
    xs = xc[:, :D_INNER]
    bm = xc[:, D_INNER:D_INNER + SSD_GROUPS * SSD_STATE]
    cm = xc[:, D_INNER + SSD_GROUPS * SSD_STATE:]

    r_i = lax.broadcasted_iota(jnp.int32, (cl, cl), 0)
    c_i = lax.broadcasted_iota(jnp.int32, (cl, cl), 1)
    causal = r_i >= c_i
    dt = _softplus(dt_ref[0] + dtb_ref[...])
    da = dt * (-jnp.exp(alog_ref[...]))
    hi = lax.Precision.HIGHEST
    acum = jnp.dot(causal.astype(F32), da, precision=hi, preferred_element_type=F32)
    wdec = jnp.exp(acum[cl - 1:cl, :] - acum) * dt
    eac = jnp.exp(acum)
    e_r = lax.broadcasted_iota(jnp.int32, (3 * LANES, D_INNER), 0)
    e_c = lax.broadcasted_iota(jnp.int32, (3 * LANES, D_INNER), 1)
    expand = (e_c // SSD_HEAD_DIM == e_r % LANES).astype(BF16)
    both = jnp.dot(jnp.concatenate([_split3(wdec), _split3(eac)], axis=0), expand, preferred_element_type=F32)
    xw = xs * both[:cl]
    eac_e = both[cl:]
    cd_e = eac_e[cl - 1:cl, :]
    acum_t = acum.T
    dt_t = dt.T

    yoffs = []
    for g in range(SSD_GROUPS):
        bg = bm[:, g * SSD_STATE:(g + 1) * SSD_STATE]
        cgb = cm[:, g * SSD_STATE:(g + 1) * SSD_STATE].astype(BF16)
        cb = lax.dot_general(cgb, bg.astype(BF16), _NT, preferred_element_type=F32)
        sg = st[g]
        yoffs.append(jnp.dot(cgb, sg.astype(BF16), preferred_element_type=F32)[:, :_GROUP_W])
        gsl = slice(g * _GROUP_W, (g + 1) * _GROUP_W)
        snew = jnp.dot(bg.T.astype(BF16), xw[:, gsl].astype(BF16), preferred_element_type=F32)
        st[g, :, 0:_GROUP_W] = cd_e[:, gsl] * sg[:, :_GROUP_W] + snew
        for r in range(SSD_HEADS_PER_GROUP):
            h = SSD_HEADS_PER_GROUP * g + r
            seg = acum[:, h:h + 1] - acum_t[h:h + 1, :]
            lmat = jnp.exp(jnp.where(causal, seg, NEG_INF))
            mm = cb * lmat * dt_t[h:h + 1, :]
            hsl = slice(h * SSD_HEAD_DIM, (h + 1) * SSD_HEAD_DIM)
            ybuf[:, hsl] = jnp.dot(mm.astype(BF16), xs[:, hsl].astype(BF16), preferred_element_type=F32)

    y = ybuf[...] + jnp.concatenate(yoffs, axis=1) * eac_e + dsk_ref[...] * xs
    z = z_ref[0]
    y = y * (z * _sigmoid(z))
    y_ref[0] = _rms(y, nrm_ref[...])

    @pl.when(c == nc - 1)
    def _():
        for g in range(SSD_GROUPS):
            t = st[g].T
            for r in range(SSD_HEADS_PER_GROUP):
                ssm_ref[0, SSD_HEADS_PER_GROUP * g + r] = t[r * SSD_HEAD_DIM:(r + 1) * SSD_HEAD_DIM, :]


_N_SSD_IN, _N_SA_IN, _N_SSD_OUT, _N_SA_OUT, _N_SSD_SCR = 11, 5, 2, 4, 3


def _ssd_and_sample_attn_kernel(n_new, *refs):
    edges = [0]
    for n in (_N_SSD_IN, _N_SA_IN, _N_SSD_OUT, _N_SA_OUT, _N_SSD_SCR, 1):
        edges.append(edges[-1] + n)
    ssd_in, sa_in, ssd_out, sa_out, ssd_scr, sa_scr = (refs[a:b] for a, b in zip(edges[:-1], edges[1:]))
    seq = pl.program_id(0) * pl.num_programs(1) + pl.program_id(1)
    for shift_slab in _sample_attn_body(n_new, seq, *sa_in, *sa_out, *sa_scr):
        shift_slab()
    _ssd_kernel(*ssd_in, *ssd_out, *ssd_scr)


def _ssd_and_sample_attn(xbc, z, dt, conv0, ssm0, params, q_s, new_t, caches_t):
    b, t, _ = xbc.shape
    cl = SSD_CHUNK
    nc = t // cl
    s, n_new, _ = q_s.shape
    assert s == b * nc
    per_tile = LANES // n_new
    cw, cb, dtb, alog, dsk, nrm = params
    tok = lambda w: pl.BlockSpec((1, cl, w), lambda bi, ci: (bi, ci, 0))
    ssm_spec = pl.BlockSpec((1, SSD_HEADS, SSD_HEAD_DIM, SSD_STATE), lambda bi, ci: (bi, 0, 0, 0))
    seq_spec = lambda shape: pl.BlockSpec(shape, lambda bi, ci: (bi * nc + ci, 0, 0))
    cache_specs = [seq_spec((1, KV_W, c.shape[2])) for c in caches_t]
    outs = pl.pallas_call(
        functools.partial(_ssd_and_sample_attn_kernel, n_new),
        grid=(b, nc),
        in_specs=[
            tok(CONV_DIM), tok(D_INNER), tok(DT_PAD),
            pl.BlockSpec((1, SUBLANES, CONV_DIM), lambda bi, ci: (bi, 0, 0)),
            ssm_spec,
            _const_spec((CONV_WIDTH, CONV_DIM)), _const_spec((1, CONV_DIM)),
            _const_spec((1, DT_PAD)), _const_spec((1, DT_PAD)),
            _const_spec((1, D_INNER)), _const_spec((1, D_INNER)),
            seq_spec((1, n_new, D_ATT)),
            pl.BlockSpec((N_BRANCH * KV_W, LANES), lambda bi, ci: (0, (bi * nc + ci) // per_tile)),
        ] + cache_specs,
        out_specs=[tok(D_INNER), ssm_spec, seq_spec((1, n_new, D_ATT))] + cache_specs,
        out_shape=[jax.ShapeDtypeStruct((b, t, D_INNER), F32),
                   jax.ShapeDtypeStruct((b, SSD_HEADS, SSD_HEAD_DIM, SSD_STATE), F32),
                   jax.ShapeDtypeStruct((s, n_new, D_ATT), F32)]
        + [jax.ShapeDtypeStruct(c.shape, F32) for c in caches_t],
        scratch_shapes=[pltpu.VMEM((SUBLANES, CONV_DIM), F32),
                        pltpu.VMEM((SSD_GROUPS, SSD_STATE, _STATE_W), F32),
                        pltpu.VMEM((cl, D_INNER), F32),
                        pltpu.VMEM((N_BRANCH * KV_W, LANES), F32)],
        compiler_params=pltpu.CompilerParams(dimension_semantics=("parallel", "arbitrary")),
        name="ssd_and_sample_attn",
    )(xbc, z, dt, conv0, ssm0, cw, cb, dtb, alog, dsk, nrm, q_s, new_t, *caches_t)
    return outs[0], outs[1], outs[2], outs[3:]


_SS_BB = 8
_HEAD_ROWS = 2 * SUBLANES


def _heads_to_features(a):
    return jnp.concatenate([jnp.broadcast_to(a[h:h + 1], (SSD_HEAD_DIM, LANES)) for h in range(SSD_HEADS)], axis=0)


def _ssd_sample_kernel(n_tok, zxt_ref, st_ref, dtt_ref, ssm0_ref, ccol_ref, hcol_ref, ycol_ref, y_ref, ssm_ref):
    group = n_tok * _SS_BB
    sub = lax.rem(pl.program_id(0), LANES // group)
    shift = lax.rem(LANES - group * sub, LANES)
    roll = lambda a, s: pltpu.roll(a, s, axis=1)
    lane = lax.broadcasted_iota(jnp.int32, (1, LANES), 1)
    tok = lane % n_tok
    seq = lane // n_tok

    zx = roll(zxt_ref[...], shift)
    zt, xt = zx[:D_INNER], zx[D_INNER:]
    stt = roll(st_ref[...].T, shift)
    dtr = roll(dtt_ref[...], shift)

    xc = ccol_ref[:, CONV_WIDTH:CONV_WIDTH + 1] + ccol_ref[:, CONV_WIDTH - 1:CONV_WIDTH] * xt
    for k in range(CONV_WIDTH - 1):
        back = CONV_WIDTH - 1 - k
        src = jnp.where(tok >= back, roll(xt, back), roll(stt, (LANES - k) % LANES))
        xc = xc + ccol_ref[:, k:k + 1] * src
    xc = xc * _sigmoid(xc)
    gn = SSD_GROUPS * SSD_STATE
    xs = xc[:D_INNER]
    bmb = xc[D_INNER:D_INNER + gn].astype(BF16)
    cmb = xc[D_INNER + gn:].astype(BF16)
    bmf, cmf = bmb.astype(F32), cmb.astype(F32)

    dt = _softplus(dtr + hcol_ref[:, 0:1])
    da = dt * (-jnp.exp(hcol_ref[:, 1:2]))
    acum = da
    for s in range(1, n_tok):
        acum = acum + jnp.where(tok >= s, roll(da, s), 0.0)
    last = jnp.where(tok == n_tok - 1, acum, 0.0)
    tot = last
    for s in range(1, n_tok):
        tot = tot + roll(last, LANES - s)
    xw = xs * _heads_to_features(jnp.exp(tot - acum) * dt)
    eac_e = _heads_to_features(jnp.exp(acum))
    cd_e = _heads_to_features(jnp.exp(tot))

    row = lax.broadcasted_iota(jnp.int32, (_HEAD_ROWS, LANES), 0)
    ydiag = None
    for d in range(n_tok):
        prod = cmf * (bmf if d == 0 else roll(bmf, d))
        cb = jnp.zeros((_HEAD_ROWS, LANES), F32)
        for g in range(SSD_GROUPS):
            cbg = jnp.sum(prod[g * SSD_STATE:(g + 1) * SSD_STATE], axis=0, keepdims=True)
            in_group = (row >= SSD_HEADS_PER_GROUP * g) & (row < SSD_HEADS_PER_GROUP * (g + 1))
            cb = jnp.where(in_group, cbg, cb)
        if d == 0:
            coef, xsh = cb * dt, xs
        else:
            ok = tok >= d
            decay = jnp.exp(jnp.where(ok, acum - roll(acum, d), 0.0))
            coef, xsh = jnp.where(ok, cb * decay * roll(dt, d), 0.0), roll(xs, d)
        term = _heads_to_features(coef) * xsh
        ydiag = term if ydiag is None else ydiag + term

    yoffs = []
    for g in range(SSD_GROUPS):
        hs = slice(SSD_HEADS_PER_GROUP * g, SSD_HEADS_PER_GROUP * (g + 1))
        rs = slice(_GROUP_W * g, _GROUP_W * (g + 1))
        cm_g, bm_g = cmb[g * SSD_STATE:(g + 1) * SSD_STATE], bmb[g * SSD_STATE:(g + 1) * SSD_STATE]
        xw_g, cd_g = xw[rs], cd_e[rs]
        h0s = [ssm0_ref[bb, hs].reshape(_GROUP_W, SSD_STATE) for bb in range(_SS_BB)]
        offs = [jnp.dot(h0.astype(BF16), cm_g, preferred_element_type=F32) for h0 in h0s]
        adds = [lax.dot_general(jnp.where(seq == bb, xw_g, 0.0).astype(BF16), bm_g, _NT, preferred_element_type=F32)
                for bb in range(_SS_BB)]
        yo = jnp.zeros((_GROUP_W, LANES), F32)
        for bb in range(_SS_BB):
            yo = jnp.where(seq == bb, offs[bb], yo)
            new = cd_g[:, n_tok * bb:n_tok * bb + 1] * h0s[bb] + adds[bb]
            ssm_ref[bb, hs] = new.reshape(SSD_HEADS_PER_GROUP, SSD_HEAD_DIM, SSD_STATE)
        yoffs.append(yo)

    y = ydiag + jnp.concatenate(yoffs, axis=0) * eac_e + ycol_ref[:, 0:1] * xs
    y = y * (zt * _sigmoid(zt))
    y = y * lax.rsqrt(jnp.mean(y * y, axis=0, keepdims=True) + RMS_EPS) * ycol_ref[:, 1:2]
    y_ref[...] = y.T[0:group, :]


def _ssd_sample(zxt, st_tok, dtt, ssm0, ccol, hcol, ycol, n_tok):
    b = ssm0.shape[0]
    m = b * n_tok
    group = n_tok * _SS_BB
    per_tile = LANES // group
    ssm_spec = pl.BlockSpec((_SS_BB, SSD_HEADS, SSD_HEAD_DIM, SSD_STATE), lambda i: (i, 0, 0, 0))
    return pl.pallas_call(
        functools.partial(_ssd_sample_kernel, n_tok),
        grid=(b // _SS_BB,),
        in_specs=[
            pl.BlockSpec((D_INNER + CONV_DIM, LANES), lambda i: (0, i // per_tile)),
            pl.BlockSpec((LANES, CONV_DIM), lambda i: (i // per_tile, 0)),
            pl.BlockSpec((_HEAD_ROWS, LANES), lambda i: (0, i // per_tile)),
            ssm_spec,
            _const_spec(ccol.shape), _const_spec(hcol.shape), _const_spec(ycol.shape),
        ],
        out_specs=[pl.BlockSpec((group, D_INNER), lambda i: (i, 0)), ssm_spec],
        out_shape=[jax.ShapeDtypeStruct((m, D_INNER), F32), jax.ShapeDtypeStruct(ssm0.shape, F32)],
        compiler_params=pltpu.CompilerParams(dimension_semantics=("parallel",)),
        name="ssd_sample",
    )(zxt, st_tok, dtt, ssm0, ccol, hcol, ycol)


def _load_tokens(ref, scr, dil):
    if dil == 1:
        return ref[0]
    n = ref.shape[1]
    halves = range(D_BRANCH // LANES)
    for r in range(dil):
        for half in halves:
            c0 = r * D_BRANCH + half * LANES
            scr[half, pl.ds(r, n, stride=dil), :] = ref[0, :, c0:c0 + LANES]
    return jnp.concatenate([scr[half] for half in halves], axis=1)


def _ffn_tail(h, gffn_ref, wg_ref, wu_ref, wd_ref, gfin_ref, y_ref, act_ref, final):
    hn = _rms(h, gffn_ref[...]).astype(BF16)
    for c in range(0, D_FF, FF_CHUNK):
        gate = jnp.dot(hn, wg_ref[:, c:c + FF_CHUNK], preferred_element_type=F32)
        up = jnp.dot(hn, wu_ref[:, c:c + FF_CHUNK], preferred_element_type=F32)
        act_ref[:, c:c + FF_CHUNK] = (gate * _sigmoid(gate) * up).astype(BF16)
    y = h + jnp.dot(act_ref[...], wd_ref[...], preferred_element_type=F32)
    y_ref[...] = _rms(y, gfin_ref[...]) if final else y


def _out_ffn_prompt_kernel(final, x_ref, *refs):
    att_refs = refs[:2 * N_BRANCH]
    ssd_ref, wo_ref, gffn_ref, wg_ref, wu_ref, wd_ref, gfin_ref, y_ref, act_ref = refs[2 * N_BRANCH:-2 * N_BRANCH]
    scrs = refs[-2 * N_BRANCH:]
    os_, ls = [], []
    for g, (win, dil) in enumerate(ATT_BRANCHES):
        os_.append(_load_tokens(att_refs[2 * g], scrs[2 * g], dil))
        ls.append(_load_tokens(att_refs[2 * g + 1], scrs[2 * g + 1], dil))
    mx = jnp.maximum(jnp.maximum(ls[0], ls[1]), ls[2])
    es = [jnp.exp(l - mx) for l in ls]
    inv = 1.0 / (es[0] + es[1] + es[2])
    h = x_ref[...]
    for g in range(N_BRANCH):
        a = (os_[g] * (es[g] * inv)).astype(BF16)
        h = h + jnp.dot(a, wo_ref[g * D_BRANCH:(g + 1) * D_BRANCH, :], preferred_element_type=F32)
    h = h + jnp.dot(ssd_ref[...].astype(BF16), wo_ref[D_ATT:D_MIX, :], preferred_element_type=F32)
    _ffn_tail(h, gffn_ref, wg_ref, wu_ref, wd_ref, gfin_ref, y_ref, act_ref, final)


def _out_ffn_sample_kernel(final, x_ref, att_ref, ssd_ref, wo_ref, gffn_ref, wg_ref, wu_ref, wd_ref, gfin_ref,
                           y_ref, act_ref):
    h = x_ref[...]
    h = h + jnp.dot(att_ref[...].astype(BF16), wo_ref[0:D_ATT, :], preferred_element_type=F32)
    h = h + jnp.dot(ssd_ref[...].astype(BF16), wo_ref[D_ATT:D_MIX, :], preferred_element_type=F32)
    _ffn_tail(h, gffn_ref, wg_ref, wu_ref, wd_ref, gfin_ref, y_ref, act_ref, final)


def _ffn_weight_specs():
    half = lambda j: pl.BlockSpec((D_MODEL, D_FF), lambda *_: (0, j), pipeline_mode=pl.Buffered(1))
    return [_const_spec((D_MIX, D_MODEL)), _const_spec((1, D_MODEL)), half(0), half(1),
            _const_spec((D_FF, D_MODEL)), _const_spec((1, D_MODEL))]


def _out_ffn_prompt(x2d, b, t, att, ssd2d, weights, final):
    m = b * t
    tm = FFN_TM
    tpb = t // tm
    tok = lambda w: pl.BlockSpec((tm, w), lambda i: (i, 0))
    att_specs = []
    for win, dil in ATT_BRANCHES:
        att_specs += [pl.BlockSpec((1, tm // dil, dil * D_BRANCH), lambda i: (i // tpb, i % tpb, 0))] * 2
    return pl.pallas_call(
        functools.partial(_out_ffn_prompt_kernel, final),
        grid=(m // tm,),
        in_specs=[tok(D_MODEL)] + att_specs + [tok(D_INNER)] + _ffn_weight_specs(),
        out_specs=tok(D_MODEL),
        out_shape=jax.ShapeDtypeStruct((m, D_MODEL), F32),
        scratch_shapes=[pltpu.VMEM((tm, D_FF), BF16)]
        + [pltpu.VMEM((D_BRANCH // LANES, tm, LANES), F32)] * (2 * N_BRANCH),
        compiler_params=pltpu.CompilerParams(dimension_semantics=("parallel",)),
        name="out_ffn_prompt",
    )(x2d, *att, ssd2d, *weights)


def _out_ffn_sample(x2d, att2d, ssd2d, weights, final):
    m = x2d.shape[0]
    tm = m // 2
    tok = lambda w: pl.BlockSpec((tm, w), lambda i: (i, 0))
    return pl.pallas_call(
        functools.partial(_out_ffn_sample_kernel, final),
        grid=(m // tm,),
        in_specs=[tok(D_MODEL), tok(D_ATT), tok(D_INNER)] + _ffn_weight_specs(),
        out_specs=tok(D_MODEL),
        out_shape=jax.ShapeDtypeStruct((m, D_MODEL), F32),
        scratch_shapes=[pltpu.VMEM((tm, D_FF), BF16)],
        compiler_params=pltpu.CompilerParams(dimension_semantics=("parallel",)),
        name="out_ffn_sample",
    )(x2d, att2d, ssd2d, *weights)


def _pad_lanes(v, width):
    return jnp.pad(v, ((0, 0), (0, width - v.shape[1])))


def kernel(x_prompt, x_sample, cache_kv_d1, cache_kv_d4, cache_kv_d16, state_conv, state_ssm, norm_mix, w_in, conv_w, conv_b, dt_bias, a_log, d_skip, norm_ssd, w_out, norm_ffn, w_gate_up, w_down, norm_final):
    bp, tp, _ = x_prompt.shape
    bs, ts, _ = x_sample.shape
    depth = w_in.shape[0]
    caches = (cache_kv_d1, cache_kv_d4, cache_kv_d16)
    hp, hs = x_prompt.reshape(bp * tp, D_MODEL), x_sample.reshape(bs * ts, D_MODEL)
    gfin = norm_final.reshape(1, D_MODEL)
    p_new, s_new = [[] for _ in range(5)], [[] for _ in range(5)]

    for l in range(depth):
        wi = w_in[l]
        wk, wv = wi[:, D_ATT:2 * D_ATT], wi[:, 2 * D_ATT:3 * D_ATT]
        wq = wi[:, :D_ATT].astype(BF16)
        wkv = jnp.concatenate([w[:, g * D_BRANCH:(g + 1) * D_BRANCH] for g in range(N_BRANCH) for w in (wk, wv)],
                              axis=1).astype(BF16)
        o = 3 * D_ATT
        wzx = wi[:, o:o + D_INNER + CONV_DIM].astype(BF16)
        wdt = _pad_lanes(wi[:, o + D_INNER + CONV_DIM:], DT_PAD).astype(BF16)
        gmix = norm_mix[l].reshape(1, D_MODEL)
        ssd_params = (conv_w[l], conv_b[l].reshape(1, CONV_DIM),
                      _pad_lanes(dt_bias[l].reshape(1, SSD_HEADS), DT_PAD),
                      _pad_lanes(a_log[l].reshape(1, SSD_HEADS), DT_PAD),
                      jnp.repeat(d_skip[l], SSD_HEAD_DIM).reshape(1, D_INNER),
                      norm_ssd[l].reshape(1, D_INNER))
        wgu = w_gate_up[l].astype(BF16)
        ffn_w = (w_out[l].astype(BF16), norm_ffn[l].reshape(1, D_MODEL), wgu, wgu, w_down[l].astype(BF16), gfin)
        final = l == depth - 1

        qkv, kts, (z, xbc, dt) = _in_proj_prompt(hp, bp, tp, gmix, wq, wkv, wzx, wdt)
        for g, (win, dil) in enumerate(ATT_BRANCHES):
            kt = kts[g].reshape(bp, 2, HEADS_PER_BRANCH, HEAD_DIM, win)
            p_new[g].append(jnp.transpose(kt, (0, 4, 1, 2, 3)))
        assert ts >= CONV_WIDTH - 1 and LANES % (ts * _SS_BB) == 0 and bs % _SS_BB == 0
        q_s, xbc_s, new_t, zxt, dtt = _in_proj_sample(hs, gmix, wq, wzx[:, D_INNER:], wkv.T, wzx.T, wdt.T)
        att = _attn_prompt(qkv, bp, tp)

        xbc3 = xbc.reshape(bp, tp, CONV_DIM)
        conv0 = jnp.zeros((bp, SUBLANES, CONV_DIM), F32)
        ssm0 = jnp.zeros((bp, SSD_HEADS, SSD_HEAD_DIM, SSD_STATE), F32)
        caches_t = [jnp.transpose(c[l], (0, 2, 3, 4, 1)).reshape(bs, KV_W, c.shape[2]) for c in caches]
        y_ssd, ssm_p, att_s, shifted = _ssd_and_sample_attn(
            xbc3, z.reshape(bp, tp, D_INNER), dt.reshape(bp, tp, DT_PAD), conv0, ssm0, ssd_params,
            q_s.reshape(bs, ts, D_ATT), new_t, caches_t)
        p_new[3].append(xbc3[:, tp - (CONV_WIDTH - 1):])
        p_new[4].append(ssm_p)
        for g in range(N_BRANCH):
            lg = shifted[g].shape[2]
            s_new[g].append(jnp.transpose(shifted[g].reshape(bs, 2, HEADS_PER_BRANCH, HEAD_DIM, lg), (0, 4, 1, 2, 3)))
        hp = _out_ffn_prompt(hp, bp, tp, att, y_ssd.reshape(bp * tp, D_INNER), ffn_w, final)

        xbc3 = xbc_s.reshape(bs, ts, CONV_DIM)
        st_tok = jnp.pad(state_conv[l], ((0, 0), (0, ts - (CONV_WIDTH - 1)), (0, 0))).reshape(bs * ts, CONV_DIM)
        pad_heads = lambda v: jnp.pad(v, (0, _HEAD_ROWS - SSD_HEADS))
        ccol = jnp.concatenate([conv_w[l].T, conv_b[l][:, None]], axis=1)
        hcol = jnp.stack([pad_heads(dt_bias[l]), pad_heads(a_log[l])], axis=1)
        ycol = jnp.stack([jnp.repeat(d_skip[l], SSD_HEAD_DIM), norm_ssd[l]], axis=1)
        y_ssd, ssm_s = _ssd_sample(zxt, st_tok, dtt, state_ssm[l], ccol, hcol, ycol, ts)
        s_new[3].append(jnp.concatenate([state_conv[l], xbc3], axis=1)[:, -(CONV_WIDTH - 1):])
        s_new[4].append(ssm_s)
        hs = _out_ffn_sample(hs, att_s.reshape(bs * ts, D_ATT), y_ssd, ffn_w, final)

    y_prompt = hp.reshape(bp, tp, D_MODEL)
    y_sample = hs.reshape(bs, ts, D_MODEL)
    p_out = [jnp.stack(a) for a in p_new]
    s_out = [jnp.stack(a) for a in s_new]
    return (y_prompt, y_sample, *p_out, *s_out)
```

```python
import functools

import jax
import jax.numpy as jnp
from jax import lax
from jax.experimental import pallas as pl
from jax.experimental.pallas import tpu as pltpu

F32 = jnp.float32
BF16 = jnp.bfloat16

D_MODEL = 1024
HEAD_DIM = 64
ATT_BRANCHES = ((128, 1), (512, 4), (2048, 16))
N_BRANCH = 3
HEADS_PER_BRANCH = 4
D_BRANCH = HEADS_PER_BRANCH * HEAD_DIM
D_ATT = N_BRANCH * D_BRANCH
SSD_HEADS = 12
SSD_HEAD_DIM = 64
D_INNER = SSD_HEADS * SSD_HEAD_DIM
SSD_GROUPS = 4
SSD_HEADS_PER_GROUP = 3
SSD_STATE = 128
CONV_WIDTH = 4
CONV_DIM = D_INNER + 2 * SSD_GROUPS * SSD_STATE
SSD_CHUNK = 128
D_MIX = D_ATT + D_INNER
D_FF = 2816
RMS_EPS = 1e-5
NEG_INF = -1e30
ATT_SCALE = HEAD_DIM ** -0.5
BAND = 128

LANES = 128
SUBLANES = 8
KV_W = 2 * D_BRANCH
DT_PAD = LANES
FF_CHUNK = 256
PROJ_CHUNK = 256
PROJ_TM = 512
ATT_QB = 2 * BAND
FFN_TM = 512

_NT = (((1,), (1,)), ((), ()))


def _rms(x, g):
    inv = lax.rsqrt(jnp.mean(x * x, axis=-1, keepdims=True) + RMS_EPS)
    return x * inv * g


def _sigmoid(x):
    return 1.0 / (1.0 + jnp.exp(-x))


def _softplus(x):
    return jnp.maximum(x, 0.0) + jnp.log1p(jnp.exp(-jnp.abs(x)))


def _const_spec(shape):
    nd = len(shape)
    return pl.BlockSpec(shape, lambda *_: (0,) * nd, pipeline_mode=pl.Buffered(1))


def _proj_store(xn, w_ref, out_ref, width):
    for c in range(0, width, PROJ_CHUNK):
        w = min(PROJ_CHUNK, width - c)
        out_ref[:, c:c + w] = jnp.dot(xn, w_ref[:, c:c + w], preferred_element_type=F32)


_Q_TILES = D_ATT // LANES
_KV_TILES = KV_W // LANES
_HALVES = D_BRANCH // LANES


def _in_proj_prompt_kernel(tiles_per_batch, x_ref, g_ref, wq_ref, wkv_ref, wzx_ref, wdt_ref, *refs):
    qkv_refs = refs[:3 * N_BRANCH]
    kt_refs = refs[3 * N_BRANCH:4 * N_BRANCH]
    z_ref, xbc_ref, dt_ref, acc = refs[4 * N_BRANCH:]
    tm = x_ref.shape[0]
    ti = lax.rem(pl.program_id(0), tiles_per_batch)
    xn = _rms(x_ref[...], g_ref[...]).astype(BF16)

    for c in range(0, D_ATT, PROJ_CHUNK):
        res = jnp.dot(xn, wq_ref[:, c:c + PROJ_CHUNK], preferred_element_type=F32) * ATT_SCALE
        for j in range(PROJ_CHUNK // LANES):
            acc[c // LANES + j] = res[:, j * LANES:(j + 1) * LANES]
    for c in range(0, N_BRANCH * KV_W, PROJ_CHUNK):
        res = jnp.dot(xn, wkv_ref[:, c:c + PROJ_CHUNK], preferred_element_type=F32)
        for j in range(PROJ_CHUNK // LANES):
            acc[_Q_TILES + c // LANES + j] = res[:, j * LANES:(j + 1) * LANES]
    _proj_store(xn, wzx_ref.at[:, 0:D_INNER], z_ref, D_INNER)
    _proj_store(xn, wzx_ref.at[:, D_INNER:D_INNER + CONV_DIM], xbc_ref, CONV_DIM)
    _proj_store(xn, wdt_ref, dt_ref, DT_PAD)

    for g, (win, dil) in enumerate(ATT_BRANCHES):
        n = tm // dil
        tiles = (_HALVES * g, _Q_TILES + _KV_TILES * g, _Q_TILES + _KV_TILES * g + _HALVES)
        for out_ref, t0 in zip(qkv_refs[3 * g:3 * g + 3], tiles):
            for r in range(dil):
                for half in range(_HALVES):
                    c0 = r * D_BRANCH + half * LANES
                    rows = acc[t0 + half] if dil == 1 else acc[t0 + half, pl.ds(r, n, stride=dil), :]
                    out_ref[0, :, c0:c0 + LANES] = rows.astype(BF16)

    for g, (win, dil) in enumerate(ATT_BRANCHES):
        w = kt_refs[g].shape[2]
        first = tiles_per_batch - win // w

        @pl.when(ti >= first)
        def _(g=g, w=w):
            for j in range(_KV_TILES):
                kt_refs[g][0, j * LANES:(j + 1) * LANES, :] = acc[_Q_TILES + _KV_TILES * g + j, tm - w:tm, :].T


def _in_proj_prompt(x2d, b, t, g, wq, wkv, wzx, wdt):
    m = b * t
    tm = PROJ_TM
    tpb = t // tm
    tok = lambda w: pl.BlockSpec((tm, w), lambda i: (i, 0))
    qkv_specs, qkv_shapes, kt_specs, kt_shapes = [], [], [], []
    for win, dil in ATT_BRANCHES:
        spec = pl.BlockSpec((1, tm // dil, dil * D_BRANCH), lambda i: (i // tpb, i % tpb, 0))
        shape = jax.ShapeDtypeStruct((b, t // dil, dil * D_BRANCH), BF16)
        qkv_specs += [spec] * 3
        qkv_shapes += [shape] * 3
        w = min(win, tm)
        first = tpb - win // w
        kt_specs.append(pl.BlockSpec((1, KV_W, w), lambda i, first=first: (i // tpb, 0, jnp.maximum(i % tpb - first, 0))))
        kt_shapes.append(jax.ShapeDtypeStruct((b, KV_W, win), F32))
    outs = pl.pallas_call(
        functools.partial(_in_proj_prompt_kernel, tpb),
        grid=(m // tm,),
        in_specs=[tok(D_MODEL), _const_spec((1, D_MODEL)), _const_spec((D_MODEL, D_ATT)),
                  _const_spec((D_MODEL, N_BRANCH * KV_W)), _const_spec((D_MODEL, D_INNER + CONV_DIM)),
                  _const_spec((D_MODEL, DT_PAD))],
        out_specs=qkv_specs + kt_specs + [tok(D_INNER), tok(CONV_DIM), tok(DT_PAD)],
        out_shape=qkv_shapes + kt_shapes + [jax.ShapeDtypeStruct((m, w), F32) for w in (D_INNER, CONV_DIM, DT_PAD)],
        scratch_shapes=[pltpu.VMEM((_Q_TILES + N_BRANCH * _KV_TILES, tm, LANES), F32)],
        compiler_params=pltpu.CompilerParams(dimension_semantics=("arbitrary",)),
        name="in_proj_prompt",
    )(x2d, g, wq, wkv, wzx, wdt)
    n = 3 * N_BRANCH
    return outs[:n], outs[n:n + N_BRANCH], outs[n + N_BRANCH:]


def _in_proj_sample_kernel(x_ref, g_ref, wq_ref, wxbc_ref, wkvt_ref, wzxt_ref, wdtt_ref,
                           q_ref, xbc_ref, kvt_ref, zxt_ref, dtt_ref):
    xn = _rms(x_ref[...], g_ref[...]).astype(BF16)
    for c in range(0, D_ATT, PROJ_CHUNK):
        q = jnp.dot(xn, wq_ref[:, c:c + PROJ_CHUNK], preferred_element_type=F32) * ATT_SCALE
        q_ref[:, c:c + PROJ_CHUNK] = q.astype(BF16).astype(F32)
    _proj_store(xn, wxbc_ref, xbc_ref, CONV_DIM)
    for wt_ref, out_ref in ((wkvt_ref, kvt_ref), (wzxt_ref, zxt_ref), (wdtt_ref, dtt_ref)):
        for r in range(0, wt_ref.shape[0], PROJ_CHUNK):
            n = min(PROJ_CHUNK, wt_ref.shape[0] - r)
            out_ref[r:r + n, :] = lax.dot_general(wt_ref[r:r + n, :], xn, _NT, preferred_element_type=F32)


def _in_proj_sample(x2d, g, wq, wxbc, wkvt, wzxt, wdtt):
    m = x2d.shape[0]
    ins = (x2d, g, wq, wxbc, wkvt, wzxt, wdtt)
    shapes = [(m, D_ATT), (m, CONV_DIM), (wkvt.shape[0], m), (wzxt.shape[0], m), (wdtt.shape[0], m)]
    return pl.pallas_call(
        _in_proj_sample_kernel,
        grid=(1,),
        in_specs=[_const_spec(a.shape) for a in ins],
        out_specs=[pl.BlockSpec(s, lambda i: (0, 0)) for s in shapes],
        out_shape=[jax.ShapeDtypeStruct(s, F32) for s in shapes],
        name="in_proj_sample",
    )(*ins)


_ATT_STEPS = 16
_ATT_ROWS = 128


def _attn_prompt_kernel(*refs):
    ins, outs = refs[:5 * N_BRANCH], refs[5 * N_BRANCH:]
    j = pl.program_id(1)
    qi = lax.broadcasted_iota(jnp.int32, (BAND, 2 * BAND), 0)
    kj = lax.broadcasted_iota(jnp.int32, (BAND, 2 * BAND), 1)
    dist = BAND + qi - kj
    in_band = (dist >= 0) & (dist <= BAND)

    tasks = []
    for g, (win, dil) in enumerate(ATT_BRANCHES):
        q_ref, kp_ref, kc_ref, vp_ref, vc_ref = ins[5 * g:5 * g + 5]
        i = lax.rem(j, _ATT_STEPS // dil)
        kc, vc = kc_ref[0], vc_ref[0]
        for sb in range(ATT_QB // BAND):
            rows = slice(sb * BAND, (sb + 1) * BAND)
            if sb == 0:
                k = jnp.concatenate([kp_ref[0], kc[rows]], axis=0)
                v = jnp.concatenate([vp_ref[0], vc[rows]], axis=0)
                valid = in_band & ((kj >= BAND) | (i > 0))
            else:
                k = kc[(sb - 1) * BAND:(sb + 1) * BAND]
                v = vc[(sb - 1) * BAND:(sb + 1) * BAND]
                valid = in_band
            for h in range(HEADS_PER_BRANCH):
                sl = slice(h * HEAD_DIM, (h + 1) * HEAD_DIM)
                for r0 in range(0, BAND, _ATT_ROWS):
                    sub = slice(sb * BAND + r0, sb * BAND + r0 + _ATT_ROWS)
                    tasks.append((q_ref[0, sub, sl], k[:, sl], v[:, sl], valid[r0:r0 + _ATT_ROWS],
                                  outs[2 * g], outs[2 * g + 1], sub, sl))

    ss = [jnp.where(t[3], lax.dot_general(t[0], t[1], _NT, preferred_element_type=F32), NEG_INF) for t in tasks]
    ms = [jnp.max(s, axis=-1, keepdims=True) for s in ss]
    ps = [jnp.exp(s - m).astype(BF16) for s, m in zip(ss, ms)]
    ones = jnp.ones((2 * BAND, HEAD_DIM), BF16)
    dens = [jnp.dot(p, ones, preferred_element_type=F32) for p in ps]
    nums = [jnp.dot(p, t[2], preferred_element_type=F32) for p, t in zip(ps, tasks)]
    for t, m, den, num in zip(tasks, ms, dens, nums):
        o_ref, l_ref, rows, sl = t[4:]
        o_ref[0, rows, sl] = num / den
        l_ref[0, rows, sl] = m + jnp.log(den)


def _attn_prompt(qkv, b, t):
    assert t // ATT_QB == _ATT_STEPS
    in_specs, args, out_specs, out_shapes = [], [], [], []
    for g, (win, dil) in enumerate(ATT_BRANCHES):
        nblk = _ATT_STEPS // dil
        cur = pl.BlockSpec((1, ATT_QB, D_BRANCH), lambda bi, j, nblk=nblk: (bi, j % nblk, j // nblk))
        prv = pl.BlockSpec((1, BAND, D_BRANCH),
                           lambda bi, j, nblk=nblk: (bi, jnp.maximum((ATT_QB // BAND) * (j % nblk) - 1, 0), j // nblk))
        q, k, v = qkv[3 * g:3 * g + 3]
        in_specs += [cur, prv, cur, prv, cur]
        args += [q, k, k, v, v]
        out_specs += [cur, cur]
        out_shapes += [jax.ShapeDtypeStruct((b, t // dil, dil * D_BRANCH), F32)] * 2
    return pl.pallas_call(
        _attn_prompt_kernel,
        grid=(b, _ATT_STEPS),
        in_specs=in_specs,
        out_specs=out_specs,
        out_shape=out_shapes,
        compiler_params=pltpu.CompilerParams(dimension_semantics=("parallel", "arbitrary")),
        name="attn_prompt",
    )(*args)


_QROWS = 4 * SUBLANES
_SHIFT_ROWS = 64


def _sample_attn_body(n_new, b, q_ref, nt_ref, c1_ref, c4_ref, c16_ref,
                      att_ref, o1_ref, o4_ref, o16_ref, nbuf):
    per_tile = LANES // n_new
    new0 = LANES - n_new
    shift = new0 - n_new * lax.rem(b, per_tile)
    nbuf[...] = pltpu.roll(nt_ref[...], shift, axis=1)

    q = q_ref[0]
    row = lax.broadcasted_iota(jnp.int32, (_QROWS, D_BRANCH), 0)
    lane = lax.broadcasted_iota(jnp.int32, (_QROWS, D_BRANCH), 1)
    hmask = (lane // HEAD_DIM) == (row % SUBLANES)

    us, dens, lses = [], [], []
    for g, ((win, dil), c_ref) in enumerate(zip(ATT_BRANCHES, (c1_ref, c4_ref, c16_ref))):
        length = c_ref.shape[2]
        qs = q[:, g * D_BRANCH:(g + 1) * D_BRANCH]
        qb = jnp.concatenate([jnp.broadcast_to(qs[s:s + 1], (SUBLANES, D_BRANCH)) for s in range(n_new)], axis=0)
        qbd = jnp.where(hmask, qb, 0.0).astype(BF16)
        kt = c_ref[0, 0:D_BRANCH, :].astype(BF16)
        vt = c_ref[0, D_BRANCH:KV_W, :].astype(BF16)
        kn = nbuf[g * KV_W:g * KV_W + D_BRANCH, :].astype(BF16)
        vn = nbuf[g * KV_W + D_BRANCH:(g + 1) * KV_W, :].astype(BF16)
        sc = jnp.dot(qbd, kt, preferred_element_type=F32)
        scn = jnp.dot(qbd, kn, preferred_element_type=F32)
        s_c = lax.broadcasted_iota(jnp.int32, (_QROWS, length), 0) // SUBLANES
        l_c = lax.broadcasted_iota(jnp.int32, (_QROWS, length), 1)
        valid_c = (l_c >= s_c) & (((l_c - s_c) & (dil - 1)) == 0)
        s_n = lax.broadcasted_iota(jnp.int32, (_QROWS, LANES), 0) // SUBLANES
        j_n = lax.broadcasted_iota(jnp.int32, (_QROWS, LANES), 1) - new0
        valid_n = (j_n >= 0) & (j_n <= s_n) & (((s_n - j_n) & (dil - 1)) == 0)
        sc = jnp.where(valid_c, sc, NEG_INF)
        scn = jnp.where(valid_n, scn, NEG_INF)
        m = jnp.maximum(jnp.max(sc, axis=-1, keepdims=True), jnp.max(scn, axis=-1, keepdims=True))
        p = jnp.exp(sc - m)
        pn = jnp.exp(scn - m)
        den = jnp.sum(p, axis=-1, keepdims=True) + jnp.sum(pn, axis=-1, keepdims=True)
        u = (lax.dot_general(p.astype(BF16), vt, _NT, preferred_element_type=F32)
             + lax.dot_general(pn.astype(BF16), vn, _NT, preferred_element_type=F32))
        us.append(u)
        dens.append(den)
        lses.append(m + jnp.log(den))

    mx = jnp.maximum(jnp.maximum(lses[0], lses[1]), lses[2])
    es = [jnp.exp(l - mx) for l in lses]
    tot = es[0] + es[1] + es[2]
    for g in range(N_BRANCH):
        coef = es[g] / tot / dens[g]
        a = jnp.where(hmask, us[g] * coef, 0.0)
        a = a.reshape(n_new, SUBLANES, D_BRANCH).sum(axis=1)
        att_ref[0, :, g * D_BRANCH:(g + 1) * D_BRANCH] = a

    def shift_slab(g, c_ref, o_ref, r0):
        keep = lax.broadcasted_iota(jnp.int32, (_SHIFT_ROWS, LANES), 1) < new0
        ntiles = c_ref.shape[2] // LANES
        rows = slice(r0, r0 + _SHIFT_ROWS)
        nxt = pltpu.roll(c_ref[0, rows, 0:LANES], new0, axis=1)
        for j in range(ntiles):
            cur = nxt
            if j + 1 < ntiles:
                nxt = pltpu.roll(c_ref[0, rows, (j + 1) * LANES:(j + 2) * LANES], new0, axis=1)
            else:
                nxt = nbuf[g * KV_W + r0:g * KV_W + r0 + _SHIFT_ROWS, :]
            o_ref[0, rows, j * LANES:(j + 1) * LANES] = jnp.where(keep, cur, nxt)

    return [functools.partial(shift_slab, g, c_ref, o_ref, r0)
            for r0 in range(0, KV_W, _SHIFT_ROWS)
            for g, (c_ref, o_ref) in enumerate(zip((c1_ref, c4_ref, c16_ref), (o1_ref, o4_ref, o16_ref)))]


_GROUP_W = SSD_HEADS_PER_GROUP * SSD_HEAD_DIM
_STATE_W = 2 * LANES
_HEAD_BATCH = 4


def _split3(a):
    a1 = a.astype(BF16)
    r1 = a - a1.astype(F32)
    a2 = r1.astype(BF16)
    a3 = (r1 - a2.astype(F32)).astype(BF16)
    return jnp.concatenate([a1, a2, a3], axis=1)


def _ssd_kernel(xbc_ref, z_ref, dt_ref, conv0_ref, ssm0_ref, cw_ref, cb_ref,
                dtb_ref, alog_ref, dsk_ref, nrm_ref, y_ref, ssm_ref, tail, st, ybuf):
    c = pl.program_id(1)
    nc = pl.num_programs(1)
    cl = SSD_CHUNK

    @pl.when(c == 0)
    def _():
        tail[...] = conv0_ref[0]
        zpad = jnp.zeros((_STATE_W - _GROUP_W, SSD_STATE), F32)
        for g in range(SSD_GROUPS):
            s = jnp.concatenate([ssm0_ref[0, SSD_HEADS_PER_GROUP * g + r] for r in range(SSD_HEADS_PER_GROUP)]
                                + [zpad], axis=0)
            st[g] = s.T

    xcur = xbc_ref[0]
    prev = tail[...]
    row8 = lax.broadcasted_iota(jnp.int32, (SUBLANES, CONV_DIM), 0)
    xc = cb_ref[...] + cw_ref[CONV_WIDTH - 1:CONV_WIDTH, :] * xcur
    for s in range(1, CONV_WIDTH):
        sh = pltpu.roll(xcur, s, axis=0)
        head = jnp.where(row8 < s, pltpu.roll(prev, s, axis=0), sh[0:SUBLANES])
        sh = jnp.concatenate([head, sh[SUBLANES:]], axis=0)
        xc = xc + cw_ref[CONV_WIDTH - 1 - s:CONV_WIDTH - s, :] * sh
    tail[...] = xcur[cl - SUBLANES:cl]
    xc = xc * _sigmoid(xc)
    xs = xc[:, :D_INNER]
    bm = xc[:, D_INNER:D_INNER + SSD_GROUPS * SSD_STATE]
    cm = xc[:, D_INNER + SSD_GROUPS * SSD_STATE:]

    r_i = lax.broadcasted_iota(jnp.int32, (cl, cl), 0)
    c_i = lax.broadcasted_iota(jnp.int32, (cl, cl), 1)
    causal = r_i >= c_i
    dt = _softplus(dt_ref[0] + dtb_ref[...])
    da = dt * (-jnp.exp(alog_ref[...]))
    hi = lax.Precision.HIGHEST
    acum = jnp.dot(causal.astype(F32), da, precision=hi, preferred_element_type=F32)
    wdec = jnp.exp(acum[cl - 1:cl, :] - acum) * dt
    eac = jnp.exp(acum)
    e_r = lax.broadcasted_iota(jnp.int32, (3 * LANES, D_INNER), 0)
    e_c = lax.broadcasted_iota(jnp.int32, (3 * LANES, D_INNER), 1)
    expand = (e_c // SSD_HEAD_DIM == e_r % LANES).astype(BF16)
    both = jnp.dot(jnp.concatenate([_split3(wdec), _split3(eac)], axis=0), expand, preferred_element_type=F32)
    xw = xs * both[:cl]
    eac_e = both[cl:]
    cd_e = eac_e[cl - 1:cl, :]
    acum_t = acum.T
    dt_t = dt.T

    groups = range(SSD_GROUPS)
    nsl = [slice(g * SSD_STATE, (g + 1) * SSD_STATE) for g in groups]
    gsl = [slice(g * _GROUP_W, (g + 1) * _GROUP_W) for g in groups]
    bgs = [bm[:, nsl[g]] for g in groups]
    cgbs = [cm[:, nsl[g]].astype(BF16) for g in groups]
    sgs = [st[g] for g in groups]
    cbs = [lax.dot_general(cgbs[g], bgs[g].astype(BF16), _NT, preferred_element_type=F32) for g in groups]
    yoffs = [jnp.dot(cgbs[g], sgs[g].astype(BF16), preferred_element_type=F32)[:, :_GROUP_W] for g in groups]
    snews = [jnp.dot(bgs[g].T.astype(BF16), xw[:, gsl[g]].astype(BF16), preferred_element_type=F32)
             for g in groups]
    for g in groups:
        st[g, :, 0:_GROUP_W] = cd_e[:, gsl[g]] * sgs[g][:, :_GROUP_W] + snews[g]
    for h0 in range(0, SSD_HEADS, _HEAD_BATCH):
        hs = range(h0, h0 + _HEAD_BATCH)
        lmats = [jnp.exp(jnp.where(causal, acum[:, h:h + 1] - acum_t[h:h + 1, :], NEG_INF)) for h in hs]
        mms = [(cbs[h // SSD_HEADS_PER_GROUP] * lm * dt_t[h:h + 1, :]).astype(BF16) for h, lm in zip(hs, lmats)]
        hsl = [slice(h * SSD_HEAD_DIM, (h + 1) * SSD_HEAD_DIM) for h in hs]
        yds = [jnp.dot(mm, xs[:, sl].astype(BF16), preferred_element_type=F32) for mm, sl in zip(mms, hsl)]
        for sl, yd in zip(hsl, yds):
            ybuf[:, sl] = yd

    y = ybuf[...] + jnp.concatenate(yoffs, axis=1) * eac_e + dsk_ref[...] * xs
    z = z_ref[0]
    y = y * (z * _sigmoid(z))
    y_ref[0] = _rms(y, nrm_ref[...])

    @pl.when(c == nc - 1)
    def _():
        for g in range(SSD_GROUPS):
            t = st[g].T
            for r in range(SSD_HEADS_PER_GROUP):
                ssm_ref[0, SSD_HEADS_PER_GROUP * g + r] = t[r * SSD_HEAD_DIM:(r + 1) * SSD_HEAD_DIM, :]


_N_SSD_IN, _N_SA_IN, _N_SSD_OUT, _N_SA_OUT, _N_SSD_SCR = 11, 5, 2, 4, 3


def _ssd_and_sample_attn_kernel(n_new, *refs):
    edges = [0]
    for n in (_N_SSD_IN, _N_SA_IN, _N_SSD_OUT, _N_SA_OUT, _N_SSD_SCR, 1):
        edges.append(edges[-1] + n)
    ssd_in, sa_in, ssd_out, sa_out, ssd_scr, sa_scr = (refs[a:b] for a, b in zip(edges[:-1], edges[1:]))
    seq = pl.program_id(0) * pl.num_programs(1) + pl.program_id(1)
    for shift_slab in _sample_attn_body(n_new, seq, *sa_in, *sa_out, *sa_scr):
        shift_slab()
    _ssd_kernel(*ssd_in, *ssd_out, *ssd_scr)


def _ssd_and_sample_attn(xbc, z, dt, conv0, ssm0, params, q_s, new_t, caches_t):
    b, t, _ = xbc.shape
    cl = SSD_CHUNK
    nc = t // cl
    s, n_new, _ = q_s.shape
    assert s == b * nc
    per_tile = LANES // n_new
    cw, cb, dtb, alog, dsk, nrm = params
    tok = lambda w: pl.BlockSpec((1, cl, w), lambda bi, ci: (bi, ci, 0))
    ssm_spec = pl.BlockSpec((1, SSD_HEADS, SSD_HEAD_DIM, SSD_STATE), lambda bi, ci: (bi, 0, 0, 0))
    seq_spec = lambda shape: pl.BlockSpec(shape, lambda bi, ci: (bi * nc + ci, 0, 0))
    cache_specs = [seq_spec((1, KV_W, c.shape[2])) for c in caches_t]
    outs = pl.pallas_call(
        functools.partial(_ssd_and_sample_attn_kernel, n_new),
        grid=(b, nc),
        in_specs=[
            tok(CONV_DIM), tok(D_INNER), tok(DT_PAD),
            pl.BlockSpec((1, SUBLANES, CONV_DIM), lambda bi, ci: (bi, 0, 0)),
            ssm_spec,
            _const_spec((CONV_WIDTH, CONV_DIM)), _const_spec((1, CONV_DIM)),
            _const_spec((1, DT_PAD)), _const_spec((1, DT_PAD)),
            _const_spec((1, D_INNER)), _const_spec((1, D_INNER)),
            seq_spec((1, n_new, D_ATT)),
            pl.BlockSpec((N_BRANCH * KV_W, LANES), lambda bi, ci: (0, (bi * nc + ci) // per_tile)),
        ] + cache_specs,
        out_specs=[tok(D_INNER), ssm_spec, seq_spec((1, n_new, D_ATT))] + cache_specs,
        out_shape=[jax.ShapeDtypeStruct((b, t, D_INNER), F32),
                   jax.ShapeDtypeStruct((b, SSD_HEADS, SSD_HEAD_DIM, SSD_STATE), F32),
                   jax.ShapeDtypeStruct((s, n_new, D_ATT), F32)]
        + [jax.ShapeDtypeStruct(c.shape, F32) for c in caches_t],
        scratch_shapes=[pltpu.VMEM((SUBLANES, CONV_DIM), F32),
                        pltpu.VMEM((SSD_GROUPS, SSD_STATE, _STATE_W), F32),
                        pltpu.VMEM((cl, D_INNER), F32),
                        pltpu.VMEM((N_BRANCH * KV_W, LANES), F32)],
        compiler_params=pltpu.CompilerParams(dimension_semantics=("parallel", "arbitrary")),
        name="ssd_and_sample_attn",
    )(xbc, z, dt, conv0, ssm0, cw, cb, dtb, alog, dsk, nrm, q_s, new_t, *caches_t)
    return outs[0], outs[1], outs[2], outs[3:]


_SS_BB = 8
_HEAD_ROWS = 2 * SUBLANES


def _heads_to_features(a):
    return jnp.concatenate([jnp.broadcast_to(a[h:h + 1], (SSD_HEAD_DIM, LANES)) for h in range(SSD_HEADS)], axis=0)


def _ssd_sample_kernel(n_tok, zxt_ref, st_ref, dtt_ref, ssm0_ref, ccol_ref, hcol_ref, ycol_ref, y_ref, ssm_ref):
    group = n_tok * _SS_BB
    sub = lax.rem(pl.program_id(0), LANES // group)
    shift = lax.rem(LANES - group * sub, LANES)
    roll = lambda a, s: pltpu.roll(a, s, axis=1)
    lane = lax.broadcasted_iota(jnp.int32, (1, LANES), 1)
    tok = lane % n_tok
    seq = lane // n_tok

    zx = roll(zxt_ref[...], shift)
    zt, xt = zx[:D_INNER], zx[D_INNER:]
    stt = roll(st_ref[...].T, shift)
    dtr = roll(dtt_ref[...], shift)

    xc = ccol_ref[CONV_WIDTH] + ccol_ref[CONV_WIDTH - 1] * xt
    for k in range(CONV_WIDTH - 1):
        back = CONV_WIDTH - 1 - k
        src = jnp.where(tok >= back, roll(xt, back), roll(stt, (LANES - k) % LANES))
        xc = xc + ccol_ref[k] * src
    xc = xc * _sigmoid(xc)
    gn = SSD_GROUPS * SSD_STATE
    xs = xc[:D_INNER]
    bmb = xc[D_INNER:D_INNER + gn].astype(BF16)
    cmb = xc[D_INNER + gn:].astype(BF16)
    bmf, cmf = bmb.astype(F32), cmb.astype(F32)

    dt = _softplus(dtr + hcol_ref[:, 0:1])
    da = dt * (-jnp.exp(hcol_ref[:, 1:2]))
    acum = da
    for s in range(1, n_tok):
        acum = acum + jnp.where(tok >= s, roll(da, s), 0.0)
    last = jnp.where(tok == n_tok - 1, acum, 0.0)
    tot = last
    for s in range(1, n_tok):
        tot = tot + roll(last, LANES - s)
    xw = xs * _heads_to_features(jnp.exp(tot - acum) * dt)
    eac_e = _heads_to_features(jnp.exp(acum))
    cd_e = _heads_to_features(jnp.exp(tot))

    row = lax.broadcasted_iota(jnp.int32, (_HEAD_ROWS, LANES), 0)
    ydiag = None
    for d in range(n_tok):
        prod = cmf * (bmf if d == 0 else roll(bmf, d))
        cb = jnp.zeros((_HEAD_ROWS, LANES), F32)
        for g in range(SSD_GROUPS):
            cbg = jnp.sum(prod[g * SSD_STATE:(g + 1) * SSD_STATE], axis=0, keepdims=True)
            in_group = (row >= SSD_HEADS_PER_GROUP * g) & (row < SSD_HEADS_PER_GROUP * (g + 1))
            cb = jnp.where(in_group, cbg, cb)
        if d == 0:
            coef, xsh = cb * dt, xs
        else:
            ok = tok >= d
            decay = jnp.exp(jnp.where(ok, acum - roll(acum, d), 0.0))
            coef, xsh = jnp.where(ok, cb * decay * roll(dt, d), 0.0), roll(xs, d)
        term = _heads_to_features(coef) * xsh
        ydiag = term if ydiag is None else ydiag + term

    yoffs = []
    for g in range(SSD_GROUPS):
        hs = slice(SSD_HEADS_PER_GROUP * g, SSD_HEADS_PER_GROUP * (g + 1))
        rs = slice(_GROUP_W * g, _GROUP_W * (g + 1))
        cm_g, bm_g = cmb[g * SSD_STATE:(g + 1) * SSD_STATE], bmb[g * SSD_STATE:(g + 1) * SSD_STATE]
        xw_g, cd_g = xw[rs], cd_e[rs]
        h0s = [ssm0_ref[bb, hs].reshape(_GROUP_W, SSD_STATE) for bb in range(_SS_BB)]
        offs = [jnp.dot(h0.astype(BF16), cm_g, preferred_element_type=F32) for h0 in h0s]
        adds = [lax.dot_general(jnp.where(seq == bb, xw_g, 0.0).astype(BF16), bm_g, _NT, preferred_element_type=F32)
                for bb in range(_SS_BB)]
        yo = jnp.zeros((_GROUP_W, LANES), F32)
        for bb in range(_SS_BB):
            yo = jnp.where(seq == bb, offs[bb], yo)
            new = cd_g[:, n_tok * bb:n_tok * bb + 1] * h0s[bb] + adds[bb]
            ssm_ref[bb, hs] = new.reshape(SSD_HEADS_PER_GROUP, SSD_HEAD_DIM, SSD_STATE)
        yoffs.append(yo)

    y = ydiag + jnp.concatenate(yoffs, axis=0) * eac_e + ycol_ref[:, 0:1] * xs
    y = y * (zt * _sigmoid(zt))
    y = y * lax.rsqrt(jnp.mean(y * y, axis=0, keepdims=True) + RMS_EPS) * ycol_ref[:, 1:2]
    y_ref[...] = y.T[0:group, :]


def _ssd_sample(zxt, st_tok, dtt, ssm0, ccol, hcol, ycol, n_tok):
    b = ssm0.shape[0]
    m = b * n_tok
    group = n_tok * _SS_BB
    per_tile = LANES // group
    ssm_spec = pl.BlockSpec((_SS_BB, SSD_HEADS, SSD_HEAD_DIM, SSD_STATE), lambda i: (i, 0, 0, 0))
    return pl.pallas_call(
        functools.partial(_ssd_sample_kernel, n_tok),
        grid=(b // _SS_BB,),
        in_specs=[
            pl.BlockSpec((D_INNER + CONV_DIM, LANES), lambda i: (0, i // per_tile)),
            pl.BlockSpec((LANES, CONV_DIM), lambda i: (i // per_tile, 0)),
            pl.BlockSpec((_HEAD_ROWS, LANES), lambda i: (0, i // per_tile)),
            ssm_spec,
            _const_spec(ccol.shape), _const_spec(hcol.shape), _const_spec(ycol.shape),
        ],
        out_specs=[pl.BlockSpec((group, D_INNER), lambda i: (i, 0)), ssm_spec],
        out_shape=[jax.ShapeDtypeStruct((m, D_INNER), F32), jax.ShapeDtypeStruct(ssm0.shape, F32)],
        compiler_params=pltpu.CompilerParams(dimension_semantics=("parallel",)),
        name="ssd_sample",
    )(zxt, st_tok, dtt, ssm0, ccol, hcol, ycol)


def _load_tokens(ref, scr, dil):
    if dil == 1:
        return ref[0]
    n = ref.shape[1]
    halves = range(D_BRANCH // LANES)
    for r in range(dil):
        for half in halves:
            c0 = r * D_BRANCH + half * LANES
            scr[half, pl.ds(r, n, stride=dil), :] = ref[0, :, c0:c0 + LANES]
    return jnp.concatenate([scr[half] for half in halves], axis=1)


def _ffn_tail(h, gffn_ref, wg_ref, wu_ref, wd_ref, gfin_ref, y_ref, act_ref, final):
    hn = _rms(h, gffn_ref[...]).astype(BF16)
    for c in range(0, D_FF, FF_CHUNK):
        gate = jnp.dot(hn, wg_ref[:, c:c + FF_CHUNK], preferred_element_type=F32)
        up = jnp.dot(hn, wu_ref[:, c:c + FF_CHUNK], preferred_element_type=F32)
        act_ref[:, c:c + FF_CHUNK] = (gate * _sigmoid(gate) * up).astype(BF16)
    y = h + jnp.dot(act_ref[...], wd_ref[...], preferred_element_type=F32)
    y_ref[...] = _rms(y, gfin_ref[...]) if final else y


def _out_ffn_prompt_kernel(final, x_ref, *refs):
    att_refs = refs[:2 * N_BRANCH]
    ssd_ref, wo_ref, gffn_ref, wg_ref, wu_ref, wd_ref, gfin_ref, y_ref, act_ref = refs[2 * N_BRANCH:-2 * N_BRANCH]
    scrs = refs[-2 * N_BRANCH:]
    os_, ls = [], []
    for g, (win, dil) in enumerate(ATT_BRANCHES):
        os_.append(_load_tokens(att_refs[2 * g], scrs[2 * g], dil))
        ls.append(_load_tokens(att_refs[2 * g + 1], scrs[2 * g + 1], dil))
    mx = jnp.maximum(jnp.maximum(ls[0], ls[1]), ls[2])
    es = [jnp.exp(l - mx) for l in ls]
    inv = 1.0 / (es[0] + es[1] + es[2])
    h = x_ref[...]
    for g in range(N_BRANCH):
        a = (os_[g] * (es[g] * inv)).astype(BF16)
        h = h + jnp.dot(a, wo_ref[g * D_BRANCH:(g + 1) * D_BRANCH, :], preferred_element_type=F32)
    h = h + jnp.dot(ssd_ref[...].astype(BF16), wo_ref[D_ATT:D_MIX, :], preferred_element_type=F32)
    _ffn_tail(h, gffn_ref, wg_ref, wu_ref, wd_ref, gfin_ref, y_ref, act_ref, final)


def _out_ffn_sample_kernel(final, x_ref, att_ref, ssd_ref, wo_ref, gffn_ref, wg_ref, wu_ref, wd_ref, gfin_ref,
                           y_ref, act_ref):
    h = x_ref[...]
    h = h + jnp.dot(att_ref[...].astype(BF16), wo_ref[0:D_ATT, :], preferred_element_type=F32)
    h = h + jnp.dot(ssd_ref[...].astype(BF16), wo_ref[D_ATT:D_MIX, :], preferred_element_type=F32)
    _ffn_tail(h, gffn_ref, wg_ref, wu_ref, wd_ref, gfin_ref, y_ref, act_ref, final)


def _ffn_weight_specs():
    half = lambda j: pl.BlockSpec((D_MODEL, D_FF), lambda *_: (0, j), pipeline_mode=pl.Buffered(1))
    return [_const_spec((D_MIX, D_MODEL)), _const_spec((1, D_MODEL)), half(0), half(1),
            _const_spec((D_FF, D_MODEL)), _const_spec((1, D_MODEL))]


def _out_ffn_prompt(x2d, b, t, att, ssd2d, weights, final):
    m = b * t
    tm = FFN_TM
    tpb = t // tm
    tok = lambda w: pl.BlockSpec((tm, w), lambda i: (i, 0))
    att_specs = []
    for win, dil in ATT_BRANCHES:
        att_specs += [pl.BlockSpec((1, tm // dil, dil * D_BRANCH), lambda i: (i // tpb, i % tpb, 0))] * 2
    return pl.pallas_call(
        functools.partial(_out_ffn_prompt_kernel, final),
        grid=(m // tm,),
        in_specs=[tok(D_MODEL)] + att_specs + [tok(D_INNER)] + _ffn_weight_specs(),
        out_specs=tok(D_MODEL),
        out_shape=jax.ShapeDtypeStruct((m, D_MODEL), F32),
        scratch_shapes=[pltpu.VMEM((tm, D_FF), BF16)]
        + [pltpu.VMEM((D_BRANCH // LANES, tm, LANES), F32)] * (2 * N_BRANCH),
        compiler_params=pltpu.CompilerParams(dimension_semantics=("parallel",)),
        name="out_ffn_prompt",
    )(x2d, *att, ssd2d, *weights)


def _out_ffn_sample(x2d, att2d, ssd2d, weights, final):
    m = x2d.shape[0]
    tm = m // 2
    tok = lambda w: pl.BlockSpec((tm, w), lambda i: (i, 0))
    return pl.pallas_call(
        functools.partial(_out_ffn_sample_kernel, final),
        grid=(m // tm,),
        in_specs=[tok(D_MODEL), tok(D_ATT), tok(D_INNER)] + _ffn_weight_specs(),
        out_specs=tok(D_MODEL),
        out_shape=jax.ShapeDtypeStruct((m, D_MODEL), F32),
        scratch_shapes=[pltpu.VMEM((tm, D_FF), BF16)],
        compiler_params=pltpu.CompilerParams(dimension_semantics=("parallel",)),
        name="out_ffn_sample",
    )(x2d, att2d, ssd2d, *weights)


def _pad_lanes(v, width):
    return jnp.pad(v, ((0, 0), (0, width - v.shape[1])))


def kernel(x_prompt, x_sample, cache_kv_d1, cache_kv_d4, cache_kv_d16, state_conv, state_ssm, norm_mix, w_in, conv_w, conv_b, dt_bias, a_log, d_skip, norm_ssd, w_out, norm_ffn, w_gate_up, w_down, norm_final):
    bp, tp, _ = x_prompt.shape
    bs, ts, _ = x_sample.shape
    depth = w_in.shape[0]
    caches = (cache_kv_d1, cache_kv_d4, cache_kv_d16)
    hp, hs = x_prompt.reshape(bp * tp, D_MODEL), x_sample.reshape(bs * ts, D_MODEL)
    gfin = norm_final.reshape(1, D_MODEL)
    p_new, s_new = [[] for _ in range(5)], [[] for _ in range(5)]

    for l in range(depth):
        wi = w_in[l]
        wk, wv = wi[:, D_ATT:2 * D_ATT], wi[:, 2 * D_ATT:3 * D_ATT]
        wq = wi[:, :D_ATT].astype(BF16)
        wkv = jnp.concatenate([w[:, g * D_BRANCH:(g + 1) * D_BRANCH] for g in range(N_BRANCH) for w in (wk, wv)],
                              axis=1).astype(BF16)
        o = 3 * D_ATT
        wzx = wi[:, o:o + D_INNER + CONV_DIM].astype(BF16)
        wdt = _pad_lanes(wi[:, o + D_INNER + CONV_DIM:], DT_PAD).astype(BF16)
        gmix = norm_mix[l].reshape(1, D_MODEL)
        ssd_params = (conv_w[l], conv_b[l].reshape(1, CONV_DIM),
                      _pad_lanes(dt_bias[l].reshape(1, SSD_HEADS), DT_PAD),
                      _pad_lanes(a_log[l].reshape(1, SSD_HEADS), DT_PAD),
                      jnp.repeat(d_skip[l], SSD_HEAD_DIM).reshape(1, D_INNER),
                      norm_ssd[l].reshape(1, D_INNER))
        wgu = w_gate_up[l].astype(BF16)
        ffn_w = (w_out[l].astype(BF16), norm_ffn[l].reshape(1, D_MODEL), wgu, wgu, w_down[l].astype(BF16), gfin)
        final = l == depth - 1

        qkv, kts, (z, xbc, dt) = _in_proj_prompt(hp, bp, tp, gmix, wq, wkv, wzx, wdt)
        for g, (win, dil) in enumerate(ATT_BRANCHES):
            kt = kts[g].reshape(bp, 2, HEADS_PER_BRANCH, HEAD_DIM, win)
            p_new[g].append(jnp.transpose(kt, (0, 4, 1, 2, 3)))
        assert ts >= CONV_WIDTH - 1 and LANES % (ts * _SS_BB) == 0 and bs % _SS_BB == 0
        q_s, xbc_s, new_t, zxt, dtt = _in_proj_sample(hs, gmix, wq, wzx[:, D_INNER:], wkv.T, wzx.T, wdt.T)
        att = _attn_prompt(qkv, bp, tp)

        xbc3 = xbc.reshape(bp, tp, CONV_DIM)
        conv0 = jnp.zeros((bp, SUBLANES, CONV_DIM), F32)
        ssm0 = jnp.zeros((bp, SSD_HEADS, SSD_HEAD_DIM, SSD_STATE), F32)
        caches_t = [jnp.transpose(c[l], (0, 2, 3, 4, 1)).reshape(bs, KV_W, c.shape[2]) for c in caches]
        y_ssd, ssm_p, att_s, shifted = _ssd_and_sample_attn(
            xbc3, z.reshape(bp, tp, D_INNER), dt.reshape(bp, tp, DT_PAD), conv0, ssm0, ssd_params,
            q_s.reshape(bs, ts, D_ATT), new_t, caches_t)
        p_new[3].append(xbc3[:, tp - (CONV_WIDTH - 1):])
        p_new[4].append(ssm_p)
        for g in range(N_BRANCH):
            lg = shifted[g].shape[2]
            s_new[g].append(jnp.transpose(shifted[g].reshape(bs, 2, HEADS_PER_BRANCH, HEAD_DIM, lg), (0, 4, 1, 2, 3)))
        hp = _out_ffn_prompt(hp, bp, tp, att, y_ssd.reshape(bp * tp, D_INNER), ffn_w, final)

        xbc3 = xbc_s.reshape(bs, ts, CONV_DIM)
        st_tok = jnp.pad(state_conv[l], ((0, 0), (0, ts - (CONV_WIDTH - 1)), (0, 0))).reshape(bs * ts, CONV_DIM)
        pad_heads = lambda v: jnp.pad(v, (0, _HEAD_ROWS - SSD_HEADS))
        ccol = jnp.broadcast_to(jnp.concatenate([conv_w[l], conv_b[l][None]], axis=0)[:, :, None],
                                (CONV_WIDTH + 1, CONV_DIM, LANES))
        hcol = jnp.stack([pad_heads(dt_bias[l]), pad_heads(a_log[l])], axis=1)
        ycol = jnp.stack([jnp.repeat(d_skip[l], SSD_HEAD_DIM), norm_ssd[l]], axis=1)
        y_ssd, ssm_s = _ssd_sample(zxt, st_tok, dtt, state_ssm[l], ccol, hcol, ycol, ts)
        s_new[3].append(jnp.concatenate([state_conv[l], xbc3], axis=1)[:, -(CONV_WIDTH - 1):])
        s_new[4].append(ssm_s)
        hs = _out_ffn_sample(hs, att_s.reshape(bs * ts, D_ATT), y_ssd, ffn_w, final)

    y_prompt = hp.reshape(bp, tp, D_MODEL)
    y_sample = hs.reshape(bs, ts, D_MODEL)
    p_out = [jnp.stack(a) for a in p_new]
    s_out = [jnp.stack(a) for a in s_new]
    return (y_prompt, y_sample, *p_out, *s_out)
```

```python
import functools

import jax
import jax.numpy as jnp
from jax import lax
from jax.experimental import pallas as pl
from jax.experimental.pallas import tpu as pltpu

F32 = jnp.float32
BF16 = jnp.bfloat16

D_MODEL = 1024
HEAD_DIM = 64
ATT_BRANCHES = ((128, 1), (512, 4), (2048, 16))
N_BRANCH = 3
HEADS_PER_BRANCH = 4
D_BRANCH = HEADS_PER_BRANCH * HEAD_DIM
D_ATT = N_BRANCH * D_BRANCH
SSD_HEADS = 12
SSD_HEAD_DIM = 64
D_INNER = SSD_HEADS * SSD_HEAD_DIM
SSD_GROUPS = 4
SSD_HEADS_PER_GROUP = 3
SSD_STATE = 128
CONV_WIDTH = 4
CONV_DIM = D_INNER + 2 * SSD_GROUPS * SSD_STATE
SSD_CHUNK = 128
D_MIX = D_ATT + D_INNER
D_FF = 2816
RMS_EPS = 1e-5
NEG_INF = -1e30
ATT_SCALE = HEAD_DIM ** -0.5
BAND = 128

LANES = 128
SUBLANES = 8
KV_W = 2 * D_BRANCH
DT_PAD = LANES
FF_CHUNK = 256
PROJ_CHUNK = 256
PROJ_TM = 512
ATT_QB = 2 * BAND
FFN_TM = 512

_NT = (((1,), (1,)), ((), ()))


def _rms(x, g):
    inv = lax.rsqrt(jnp.mean(x * x, axis=-1, keepdims=True) + RMS_EPS)
    return x * inv * g


def _sigmoid(x):
    return 1.0 / (1.0 + jnp.exp(-x))


def _softplus(x):
    return jnp.maximum(x, 0.0) + jnp.log1p(jnp.exp(-jnp.abs(x)))


def _const_spec(shape):
    nd = len(shape)
    return pl.BlockSpec(shape, lambda *_: (0,) * nd, pipeline_mode=pl.Buffered(1))


def _proj_store(xn, w_ref, out_ref, width):
    for c in range(0, width, PROJ_CHUNK):
        w = min(PROJ_CHUNK, width - c)
        out_ref[:, c:c + w] = jnp.dot(xn, w_ref[:, c:c + w], preferred_element_type=F32)


_Q_TILES = D_ATT // LANES
_KV_TILES = KV_W // LANES
_HALVES = D_BRANCH // LANES


def _in_proj_prompt_kernel(tiles_per_batch, x_ref, g_ref, wq_ref, wkv_ref, wzx_ref, wdt_ref, *refs):
    qkv_refs = refs[:3 * N_BRANCH]
    kt_refs = refs[3 * N_BRANCH:4 * N_BRANCH]
    z_ref, xbc_ref, dt_ref, acc = refs[4 * N_BRANCH:]
    tm = x_ref.shape[0]
    ti = lax.rem(pl.program_id(0), tiles_per_batch)
    xn = _rms(x_ref[...], g_ref[...]).astype(BF16)

    for c in range(0, D_ATT, PROJ_CHUNK):
        res = jnp.dot(xn, wq_ref[:, c:c + PROJ_CHUNK], preferred_element_type=F32) * ATT_SCALE
        for j in range(PROJ_CHUNK // LANES):
            acc[c // LANES + j] = res[:, j * LANES:(j + 1) * LANES]
    for c in range(0, N_BRANCH * KV_W, PROJ_CHUNK):
        res = jnp.dot(xn, wkv_ref[:, c:c + PROJ_CHUNK], preferred_element_type=F32)
        for j in range(PROJ_CHUNK // LANES):
            acc[_Q_TILES + c // LANES + j] = res[:, j * LANES:(j + 1) * LANES]
    _proj_store(xn, wzx_ref.at[:, 0:D_INNER], z_ref, D_INNER)
    _proj_store(xn, wzx_ref.at[:, D_INNER:D_INNER + CONV_DIM], xbc_ref, CONV_DIM)
    _proj_store(xn, wdt_ref, dt_ref, DT_PAD)

    for g, (win, dil) in enumerate(ATT_BRANCHES):
        n = tm // dil
        tiles = (_HALVES * g, _Q_TILES + _KV_TILES * g, _Q_TILES + _KV_TILES * g + _HALVES)
        for out_ref, t0 in zip(qkv_refs[3 * g:3 * g + 3], tiles):
            for r in range(dil):
                for half in range(_HALVES):
                    c0 = r * D_BRANCH + half * LANES
                    rows = acc[t0 + half] if dil == 1 else acc[t0 + half, pl.ds(r, n, stride=dil), :]
                    out_ref[0, :, c0:c0 + LANES] = rows.astype(BF16)

    for g, (win, dil) in enumerate(ATT_BRANCHES):
        w = kt_refs[g].shape[2]
        first = tiles_per_batch - win // w

        @pl.when(ti >= first)
        def _(g=g, w=w):
            for j in range(_KV_TILES):
                kt_refs[g][0, j * LANES:(j + 1) * LANES, :] = acc[_Q_TILES + _KV_TILES * g + j, tm - w:tm, :].T


def _in_proj_prompt(x2d, b, t, g, wq, wkv, wzx, wdt):
    m = b * t
    tm = PROJ_TM
    tpb = t // tm
    tok = lambda w: pl.BlockSpec((tm, w), lambda i: (i, 0))
    qkv_specs, qkv_shapes, kt_specs, kt_shapes = [], [], [], []
    for win, dil in ATT_BRANCHES:
        spec = pl.BlockSpec((1, tm // dil, dil * D_BRANCH), lambda i: (i // tpb, i % tpb, 0))
        shape = jax.ShapeDtypeStruct((b, t // dil, dil * D_BRANCH), BF16)
        qkv_specs += [spec] * 3
        qkv_shapes += [shape] * 3
        w = min(win, tm)
        first = tpb - win // w
        kt_specs.append(pl.BlockSpec((1, KV_W, w), lambda i, first=first: (i // tpb, 0, jnp.maximum(i % tpb - first, 0))))
        kt_shapes.append(jax.ShapeDtypeStruct((b, KV_W, win), F32))
    outs = pl.pallas_call(
        functools.partial(_in_proj_prompt_kernel, tpb),
        grid=(m // tm,),
        in_specs=[tok(D_MODEL), _const_spec((1, D_MODEL)), _const_spec((D_MODEL, D_ATT)),
                  _const_spec((D_MODEL, N_BRANCH * KV_W)), _const_spec((D_MODEL, D_INNER + CONV_DIM)),
                  _const_spec((D_MODEL, DT_PAD))],
        out_specs=qkv_specs + kt_specs + [tok(D_INNER), tok(CONV_DIM), tok(DT_PAD)],
        out_shape=qkv_shapes + kt_shapes + [jax.ShapeDtypeStruct((m, w), F32) for w in (D_INNER, CONV_DIM, DT_PAD)],
        scratch_shapes=[pltpu.VMEM((_Q_TILES + N_BRANCH * _KV_TILES, tm, LANES), F32)],
        compiler_params=pltpu.CompilerParams(dimension_semantics=("arbitrary",)),
        name="in_proj_prompt",
    )(x2d, g, wq, wkv, wzx, wdt)
    n = 3 * N_BRANCH
    return outs[:n], outs[n:n + N_BRANCH], outs[n + N_BRANCH:]


def _in_proj_sample_kernel(x_ref, g_ref, wq_ref, wzx_ref, wkvt_ref, wzxt_ref, wdtt_ref,
                           q_ref, xbc_ref, kvt_ref, zxt_ref, dtt_ref):
    xn = _rms(x_ref[...], g_ref[...]).astype(BF16)
    for c in range(0, D_ATT, PROJ_CHUNK):
        q = jnp.dot(xn, wq_ref[:, c:c + PROJ_CHUNK], preferred_element_type=F32) * ATT_SCALE
        q_ref[:, c:c + PROJ_CHUNK] = q.astype(BF16).astype(F32)
    _proj_store(xn, wzx_ref.at[:, D_INNER:D_INNER + CONV_DIM], xbc_ref, CONV_DIM)
    for wt_ref, out_ref in ((wkvt_ref, kvt_ref), (wzxt_ref, zxt_ref), (wdtt_ref, dtt_ref)):
        for r in range(0, wt_ref.shape[0], PROJ_CHUNK):
            n = min(PROJ_CHUNK, wt_ref.shape[0] - r)
            out_ref[r:r + n, :] = lax.dot_general(wt_ref[r:r + n, :], xn, _NT, preferred_element_type=F32)


def _in_proj_sample(x2d, g, wq, wzx, wkvt, wzxt, wdtt):
    m = x2d.shape[0]
    ins = (x2d, g, wq, wzx, wkvt, wzxt, wdtt)
    shapes = [(m, D_ATT), (m, CONV_DIM), (wkvt.shape[0], m), (wzxt.shape[0], m), (wdtt.shape[0], m)]
    return pl.pallas_call(
        _in_proj_sample_kernel,
        grid=(1,),
        in_specs=[_const_spec(a.shape) for a in ins],
        out_specs=[pl.BlockSpec(s, lambda i: (0, 0)) for s in shapes],
        out_shape=[jax.ShapeDtypeStruct(s, F32) for s in shapes],
        name="in_proj_sample",
    )(*ins)


_ATT_STEPS = 16
_ATT_ROWS = 128


def _attn_prompt_kernel(*refs):
    ins, outs = refs[:5 * N_BRANCH], refs[5 * N_BRANCH:]
    j = pl.program_id(1)
    qi = lax.broadcasted_iota(jnp.int32, (BAND, 2 * BAND), 0)
    kj = lax.broadcasted_iota(jnp.int32, (BAND, 2 * BAND), 1)
    dist = BAND + qi - kj
    in_band = (dist >= 0) & (dist <= BAND)

    tasks = []
    for g, (win, dil) in enumerate(ATT_BRANCHES):
        q_ref, kp_ref, kc_ref, vp_ref, vc_ref = ins[5 * g:5 * g + 5]
        i = lax.rem(j, _ATT_STEPS // dil)
        kc, vc = kc_ref[0], vc_ref[0]
        for sb in range(ATT_QB // BAND):
            rows = slice(sb * BAND, (sb + 1) * BAND)
            if sb == 0:
                k = jnp.concatenate([kp_ref[0], kc[rows]], axis=0)
                v = jnp.concatenate([vp_ref[0], vc[rows]], axis=0)
                valid = in_band & ((kj >= BAND) | (i > 0))
            else:
                k = kc[(sb - 1) * BAND:(sb + 1) * BAND]
                v = vc[(sb - 1) * BAND:(sb + 1) * BAND]
                valid = in_band
            for h in range(HEADS_PER_BRANCH):
                sl = slice(h * HEAD_DIM, (h + 1) * HEAD_DIM)
                for r0 in range(0, BAND, _ATT_ROWS):
                    sub = slice(sb * BAND + r0, sb * BAND + r0 + _ATT_ROWS)
                    tasks.append((q_ref[0, sub, sl], k[:, sl], v[:, sl], valid[r0:r0 + _ATT_ROWS],
                                  outs[2 * g], outs[2 * g + 1], sub, sl))

    ss = [jnp.where(t[3], lax.dot_general(t[0], t[1], _NT, preferred_element_type=F32), NEG_INF) for t in tasks]
    ms = [jnp.max(s, axis=-1, keepdims=True) for s in ss]
    ps = [jnp.exp(s - m).astype(BF16) for s, m in zip(ss, ms)]
    ones = jnp.ones((2 * BAND, HEAD_DIM), BF16)
    dens = [jnp.dot(p, ones, preferred_element_type=F32) for p in ps]
    nums = [jnp.dot(p, t[2], preferred_element_type=F32) for p, t in zip(ps, tasks)]
    for t, m, den, num in zip(tasks, ms, dens, nums):
        o_ref, l_ref, rows, sl = t[4:]
        o_ref[0, rows, sl] = num / den
        l_ref[0, rows, sl] = m + jnp.log(den)


def _attn_prompt(qkv, b, t):
    assert t // ATT_QB == _ATT_STEPS
    in_specs, args, out_specs, out_shapes = [], [], [], []
    for g, (win, dil) in enumerate(ATT_BRANCHES):
        nblk = _ATT_STEPS // dil
        cur = pl.BlockSpec((1, ATT_QB, D_BRANCH), lambda bi, j, nblk=nblk: (bi, j % nblk, j // nblk))
        prv = pl.BlockSpec((1, BAND, D_BRANCH),
                           lambda bi, j, nblk=nblk: (bi, jnp.maximum((ATT_QB // BAND) * (j % nblk) - 1, 0), j // nblk))
        q, k, v = qkv[3 * g:3 * g + 3]
        in_specs += [cur, prv, cur, prv, cur]
        args += [q, k, k, v, v]
        out_specs += [cur, cur]
        out_shapes += [jax.ShapeDtypeStruct((b, t // dil, dil * D_BRANCH), F32)] * 2
    return pl.pallas_call(
        _attn_prompt_kernel,
        grid=(b, _ATT_STEPS),
        in_specs=in_specs,
        out_specs=out_specs,
        out_shape=out_shapes,
        compiler_params=pltpu.CompilerParams(dimension_semantics=("parallel", "arbitrary")),
        name="attn_prompt",
    )(*args)


_QROWS = 4 * SUBLANES
_SHIFT_ROWS = 64


def _sample_attn_body(n_new, b, q_ref, nt_ref, c1_ref, c4_ref, c16_ref,
                      att_ref, o1_ref, o4_ref, o16_ref, nbuf):
    per_tile = LANES // n_new
    new0 = LANES - n_new
    shift = new0 - n_new * lax.rem(b, per_tile)
    nbuf[...] = pltpu.roll(nt_ref[...], shift, axis=1)

    q = q_ref[0]
    row = lax.broadcasted_iota(jnp.int32, (_QROWS, D_BRANCH), 0)
    lane = lax.broadcasted_iota(jnp.int32, (_QROWS, D_BRANCH), 1)
    hmask = (lane // HEAD_DIM) == (row % SUBLANES)

    branches = range(N_BRANCH)
    c_refs = (c1_ref, c4_ref, c16_ref)
    dils = [dil for _, dil in ATT_BRANCHES]
    qbds = []
    for g in branches:
        qs = q[:, g * D_BRANCH:(g + 1) * D_BRANCH]
        qb = jnp.concatenate([jnp.broadcast_to(qs[s:s + 1], (SUBLANES, D_BRANCH)) for s in range(n_new)], axis=0)
        qbds.append(jnp.where(hmask, qb, 0.0).astype(BF16))
    kts = [c_refs[g][0, 0:D_BRANCH, :].astype(BF16) for g in branches]
    kns = [nbuf[g * KV_W:g * KV_W + D_BRANCH, :].astype(BF16) for g in branches]
    scs = [jnp.dot(qbds[g], kts[g], preferred_element_type=F32) for g in branches]
    scns = [jnp.dot(qbds[g], kns[g], preferred_element_type=F32) for g in branches]
    vts = [c_refs[g][0, D_BRANCH:KV_W, :].astype(BF16) for g in branches]
    vns = [nbuf[g * KV_W + D_BRANCH:(g + 1) * KV_W, :].astype(BF16) for g in branches]
    s_n = lax.broadcasted_iota(jnp.int32, (_QROWS, LANES), 0) // SUBLANES
    j_n = lax.broadcasted_iota(jnp.int32, (_QROWS, LANES), 1) - new0
    for g in branches:
        length = c_refs[g].shape[2]
        s_c = lax.broadcasted_iota(jnp.int32, (_QROWS, length), 0) // SUBLANES
        l_c = lax.broadcasted_iota(jnp.int32, (_QROWS, length), 1)
        valid_c = (l_c >= s_c) & (((l_c - s_c) & (dils[g] - 1)) == 0)
        valid_n = (j_n >= 0) & (j_n <= s_n) & (((s_n - j_n) & (dils[g] - 1)) == 0)
        scs[g] = jnp.where(valid_c, scs[g], NEG_INF)
        scns[g] = jnp.where(valid_n, scns[g], NEG_INF)
    ms = [jnp.maximum(jnp.max(scs[g], axis=-1, keepdims=True), jnp.max(scns[g], axis=-1, keepdims=True))
          for g in branches]
    ps = [jnp.exp(scs[g] - ms[g]) for g in branches]
    pns = [jnp.exp(scns[g] - ms[g]) for g in branches]
    dens = [jnp.sum(ps[g], axis=-1, keepdims=True) + jnp.sum(pns[g], axis=-1, keepdims=True) for g in branches]
    us = [lax.dot_general(ps[g].astype(BF16), vts[g], _NT, preferred_element_type=F32)
          + lax.dot_general(pns[g].astype(BF16), vns[g], _NT, preferred_element_type=F32) for g in branches]
    lses = [ms[g] + jnp.log(dens[g]) for g in branches]

    mx = jnp.maximum(jnp.maximum(lses[0], lses[1]), lses[2])
    es = [jnp.exp(l - mx) for l in lses]
    tot = es[0] + es[1] + es[2]
    for g in range(N_BRANCH):
        coef = es[g] / tot / dens[g]
        a = jnp.where(hmask, us[g] * coef, 0.0)
        a = a.reshape(n_new, SUBLANES, D_BRANCH).sum(axis=1)
        att_ref[0, :, g * D_BRANCH:(g + 1) * D_BRANCH] = a

    def shift_slab(g, c_ref, o_ref, r0):
        keep = lax.broadcasted_iota(jnp.int32, (_SHIFT_ROWS, LANES), 1) < new0
        ntiles = c_ref.shape[2] // LANES
        rows = slice(r0, r0 + _SHIFT_ROWS)
        nxt = pltpu.roll(c_ref[0, rows, 0:LANES], new0, axis=1)
        for j in range(ntiles):
            cur = nxt
            if j + 1 < ntiles:
                nxt = pltpu.roll(c_ref[0, rows, (j + 1) * LANES:(j + 2) * LANES], new0, axis=1)
            else:
                nxt = nbuf[g * KV_W + r0:g * KV_W + r0 + _SHIFT_ROWS, :]
            o_ref[0, rows, j * LANES:(j + 1) * LANES] = jnp.where(keep, cur, nxt)

    return [functools.partial(shift_slab, g, c_ref, o_ref, r0)
            for r0 in range(0, KV_W, _SHIFT_ROWS)
            for g, (c_ref, o_ref) in enumerate(zip((c1_ref, c4_ref, c16_ref), (o1_ref, o4_ref, o16_ref)))]


_GROUP_W = SSD_HEADS_PER_GROUP * SSD_HEAD_DIM
_STATE_W = 2 * LANES
_HEAD_BATCH = 4


def _split3(a):
    a1 = a.astype(BF16)
    r1 = a - a1.astype(F32)
    a2 = r1.astype(BF16)
    a3 = (r1 - a2.astype(F32)).astype(BF16)
    return jnp.concatenate([a1, a2, a3], axis=1)


def _ssd_kernel(xbc_ref, z_ref, dt_ref, conv0_ref, ssm0_ref, cw_ref, cb_ref,
                dtb_ref, alog_ref, dsk_ref, nrm_ref, y_ref, ssm_ref, tail, st, ybuf):
    c = pl.program_id(1)
    nc = pl.num_programs(1)
    cl = SSD_CHUNK

    @pl.when(c == 0)
    def _():
        tail[...] = conv0_ref[0]
        zpad = jnp.zeros((_STATE_W - _GROUP_W, SSD_STATE), F32)
        for g in range(SSD_GROUPS):
            s = jnp.concatenate([ssm0_ref[0, SSD_HEADS_PER_GROUP * g + r] for r in range(SSD_HEADS_PER_GROUP)]
                                + [zpad], axis=0)
            st[g] = s.T

    xcur = xbc_ref[0]
    prev = tail[...]
    row8 = lax.broadcasted_iota(jnp.int32, (SUBLANES, CONV_DIM), 0)
    xc = cb_ref[...] + cw_ref[CONV_WIDTH - 1:CONV_WIDTH, :] * xcur
    for s in range(1, CONV_WIDTH):
        sh = pltpu.roll(xcur, s, axis=0)
        head = jnp.where(row8 < s, pltpu.roll(prev, s, axis=0), sh[0:SUBLANES])
        sh = jnp.concatenate([head, sh[SUBLANES:]], axis=0)
        xc = xc + cw_ref[CONV_WIDTH - 1 - s:CONV_WIDTH - s, :] * sh
    tail[...] = xcur[cl - SUBLANES:cl]
    xc = xc * _sigmoid(xc)
    xs = xc[:, :D_INNER]
    bm = xc[:, D_INNER:D_INNER + SSD_GROUPS * SSD_STATE]
    cm = xc[:, D_INNER + SSD_GROUPS * SSD_STATE:]

    r_i = lax.broadcasted_iota(jnp.int32, (cl, cl), 0)
    c_i = lax.broadcasted_iota(jnp.int32, (cl, cl), 1)
    causal = r_i >= c_i
    dt = _softplus(dt_ref[0] + dtb_ref[...])
    da = dt * (-jnp.exp(alog_ref[...]))
    parts = jnp.dot(causal.astype(BF16), _split3(da), preferred_element_type=F32)
    acum = parts[:, :LANES] + parts[:, LANES:2 * LANES] + parts[:, 2 * LANES:]
    wdec = jnp.exp(acum[cl - 1:cl, :] - acum) * dt
    eac = jnp.exp(acum)
    e_r = lax.broadcasted_iota(jnp.int32, (3 * LANES, D_INNER), 0)
    e_c = lax.broadcasted_iota(jnp.int32, (3 * LANES, D_INNER), 1)
    expand = (e_c // SSD_HEAD_DIM == e_r % LANES).astype(BF16)
    both = jnp.dot(jnp.concatenate([_split3(wdec), _split3(eac)], axis=0), expand, preferred_element_type=F32)
    xw = xs * both[:cl]
    eac_e = both[cl:]
    cd_e = eac_e[cl - 1:cl, :]
    acum_t = acum.T
    dt_t = dt.T

    groups = range(SSD_GROUPS)
    nsl = [slice(g * SSD_STATE, (g + 1) * SSD_STATE) for g in groups]
    gsl = [slice(g * _GROUP_W, (g + 1) * _GROUP_W) for g in groups]
    bgs = [bm[:, nsl[g]] for g in groups]
    cgbs = [cm[:, nsl[g]].astype(BF16) for g in groups]
    sgs = [st[g] for g in groups]
    cbs = [lax.dot_general(cgbs[g], bgs[g].astype(BF16), _NT, preferred_element_type=F32) for g in groups]
    yoffs = [jnp.dot(cgbs[g], sgs[g].astype(BF16), preferred_element_type=F32)[:, :_GROUP_W] for g in groups]
    snews = [jnp.dot(bgs[g].T.astype(BF16), xw[:, gsl[g]].astype(BF16), preferred_element_type=F32)
             for g in groups]
    for g in groups:
        st[g, :, 0:_GROUP_W] = cd_e[:, gsl[g]] * sgs[g][:, :_GROUP_W] + snews[g]
    for h0 in range(0, SSD_HEADS, _HEAD_BATCH):
        hs = range(h0, h0 + _HEAD_BATCH)
        lmats = [jnp.exp(jnp.where(causal, acum[:, h:h + 1] - acum_t[h:h + 1, :], NEG_INF)) for h in hs]
        mms = [(cbs[h // SSD_HEADS_PER_GROUP] * lm * dt_t[h:h + 1, :]).astype(BF16) for h, lm in zip(hs, lmats)]
        hsl = [slice(h * SSD_HEAD_DIM, (h + 1) * SSD_HEAD_DIM) for h in hs]
        yds = [jnp.dot(mm, xs[:, sl].astype(BF16), preferred_element_type=F32) for mm, sl in zip(mms, hsl)]
        for sl, yd in zip(hsl, yds):
            ybuf[:, sl] = yd

    y = ybuf[...] + jnp.concatenate(yoffs, axis=1) * eac_e + dsk_ref[...] * xs
    z = z_ref[0]
    y = y * (z * _sigmoid(z))
    y_ref[0] = _rms(y, nrm_ref[...])

    @pl.when(c == nc - 1)
    def _():
        for g in range(SSD_GROUPS):
            t = st[g].T
            for r in range(SSD_HEADS_PER_GROUP):
                ssm_ref[0, SSD_HEADS_PER_GROUP * g + r] = t[r * SSD_HEAD_DIM:(r + 1) * SSD_HEAD_DIM, :]


_N_SSD_IN, _N_SA_IN, _N_SSD_OUT, _N_SA_OUT, _N_SSD_SCR = 11, 5, 2, 4, 3


def _ssd_and_sample_attn_kernel(n_new, *refs):
    edges = [0]
    for n in (_N_SSD_IN, _N_SA_IN, _N_SSD_OUT, _N_SA_OUT, _N_SSD_SCR, 1):
        edges.append(edges[-1] + n)
    ssd_in, sa_in, ssd_out, sa_out, ssd_scr, sa_scr = (refs[a:b] for a, b in zip(edges[:-1], edges[1:]))
    seq = pl.program_id(0) * pl.num_programs(1) + pl.program_id(1)
    for shift_slab in _sample_attn_body(n_new, seq, *sa_in, *sa_out, *sa_scr):
        shift_slab()
    _ssd_kernel(*ssd_in, *ssd_out, *ssd_scr)


def _ssd_and_sample_attn(xbc, z, dt, conv0, ssm0, params, q_s, new_t, caches_t):
    b, t, _ = xbc.shape
    cl = SSD_CHUNK
    nc = t // cl
    s, n_new, _ = q_s.shape
    assert s == b * nc
    per_tile = LANES // n_new
    cw, cb, dtb, alog, dsk, nrm = params
    tok = lambda w: pl.BlockSpec((1, cl, w), lambda bi, ci: (bi, ci, 0))
    ssm_spec = pl.BlockSpec((1, SSD_HEADS, SSD_HEAD_DIM, SSD_STATE), lambda bi, ci: (bi, 0, 0, 0))
    seq_spec = lambda shape: pl.BlockSpec(shape, lambda bi, ci: (bi * nc + ci, 0, 0))
    cache_specs = [seq_spec((1, KV_W, c.shape[2])) for c in caches_t]
    outs = pl.pallas_call(
        functools.partial(_ssd_and_sample_attn_kernel, n_new),
        grid=(b, nc),
        in_specs=[
            tok(CONV_DIM), tok(D_INNER), tok(DT_PAD),
            pl.BlockSpec((1, SUBLANES, CONV_DIM), lambda bi, ci: (bi, 0, 0)),
            ssm_spec,
            _const_spec((CONV_WIDTH, CONV_DIM)), _const_spec((1, CONV_DIM)),
            _const_spec((1, DT_PAD)), _const_spec((1, DT_PAD)),
            _const_spec((1, D_INNER)), _const_spec((1, D_INNER)),
            seq_spec((1, n_new, D_ATT)),
            pl.BlockSpec((N_BRANCH * KV_W, LANES), lambda bi, ci: (0, (bi * nc + ci) // per_tile)),
        ] + cache_specs,
        out_specs=[tok(D_INNER), ssm_spec, seq_spec((1, n_new, D_ATT))] + cache_specs,
        out_shape=[jax.ShapeDtypeStruct((b, t, D_INNER), F32),
                   jax.ShapeDtypeStruct((b, SSD_HEADS, SSD_HEAD_DIM, SSD_STATE), F32),
                   jax.ShapeDtypeStruct((s, n_new, D_ATT), F32)]
        + [jax.ShapeDtypeStruct(c.shape, F32) for c in caches_t],
        scratch_shapes=[pltpu.VMEM((SUBLANES, CONV_DIM), F32),
                        pltpu.VMEM((SSD_GROUPS, SSD_STATE, _STATE_W), F32),
                        pltpu.VMEM((cl, D_INNER), F32),
                        pltpu.VMEM((N_BRANCH * KV_W, LANES), F32)],
        compiler_params=pltpu.CompilerParams(dimension_semantics=("parallel", "arbitrary")),
        name="ssd_and_sample_attn",
    )(xbc, z, dt, conv0, ssm0, cw, cb, dtb, alog, dsk, nrm, q_s, new_t, *caches_t)
    return outs[0], outs[1], outs[2], outs[3:]


_SS_BB = 8
_HEAD_ROWS = 2 * SUBLANES


def _heads_to_features(a):
    return jnp.concatenate([jnp.broadcast_to(a[h:h + 1], (SSD_HEAD_DIM, LANES)) for h in range(SSD_HEADS)], axis=0)


def _ssd_sample_kernel(n_tok, zxt_ref, st_ref, dtt_ref, ssm0_ref, ccol_ref, hcol_ref, ycol_ref, y_ref, ssm_ref):
    group = n_tok * _SS_BB
    sub = lax.rem(pl.program_id(0), LANES // group)
    shift = lax.rem(LANES - group * sub, LANES)
    roll = lambda a, s: pltpu.roll(a, s, axis=1)
    lane = lax.broadcasted_iota(jnp.int32, (1, LANES), 1)
    tok = lane % n_tok
    seq = lane // n_tok

    zx = roll(zxt_ref[...], shift)
    zt, xt = zx[:D_INNER], zx[D_INNER:]
    stt = roll(st_ref[...].T, shift)
    dtr = roll(dtt_ref[...], shift)

    xc = ccol_ref[CONV_WIDTH] + ccol_ref[CONV_WIDTH - 1] * xt
    for k in range(CONV_WIDTH - 1):
        back = CONV_WIDTH - 1 - k
        src = jnp.where(tok >= back, roll(xt, back), roll(stt, (LANES - k) % LANES))
        xc = xc + ccol_ref[k] * src
    xc = xc * _sigmoid(xc)
    gn = SSD_GROUPS * SSD_STATE
    xs = xc[:D_INNER]
    bmb = xc[D_INNER:D_INNER + gn].astype(BF16)
    cmb = xc[D_INNER + gn:].astype(BF16)
    bmf, cmf = bmb.astype(F32), cmb.astype(F32)

    dt = _softplus(dtr + hcol_ref[:, 0:1])
    da = dt * (-jnp.exp(hcol_ref[:, 1:2]))
    acum = da
    for s in range(1, n_tok):
        acum = acum + jnp.where(tok >= s, roll(da, s), 0.0)
    last = jnp.where(tok == n_tok - 1, acum, 0.0)
    tot = last
    for s in range(1, n_tok):
        tot = tot + roll(last, LANES - s)
    xw = xs * _heads_to_features(jnp.exp(tot - acum) * dt)
    eac_e = _heads_to_features(jnp.exp(acum))
    cd_e = _heads_to_features(jnp.exp(tot))

    row = lax.broadcasted_iota(jnp.int32, (_HEAD_ROWS, LANES), 0)
    ydiag = None
    for d in range(n_tok):
        prod = cmf * (bmf if d == 0 else roll(bmf, d))
        cb = jnp.zeros((_HEAD_ROWS, LANES), F32)
        for g in range(SSD_GROUPS):
            cbg = jnp.sum(prod[g * SSD_STATE:(g + 1) * SSD_STATE], axis=0, keepdims=True)
            in_group = (row >= SSD_HEADS_PER_GROUP * g) & (row < SSD_HEADS_PER_GROUP * (g + 1))
            cb = jnp.where(in_group, cbg, cb)
        if d == 0:
            coef, xsh = cb * dt, xs
        else:
            ok = tok >= d
            decay = jnp.exp(jnp.where(ok, acum - roll(acum, d), 0.0))
            coef, xsh = jnp.where(ok, cb * decay * roll(dt, d), 0.0), roll(xs, d)
        term = _heads_to_features(coef) * xsh
        ydiag = term if ydiag is None else ydiag + term

    yoffs = []
    for g in range(SSD_GROUPS):
        hs = slice(SSD_HEADS_PER_GROUP * g, SSD_HEADS_PER_GROUP * (g + 1))
        rs = slice(_GROUP_W * g, _GROUP_W * (g + 1))
        cm_g, bm_g = cmb[g * SSD_STATE:(g + 1) * SSD_STATE], bmb[g * SSD_STATE:(g + 1) * SSD_STATE]
        xw_g, cd_g = xw[rs], cd_e[rs]
        h0s = [ssm0_ref[bb, hs].reshape(_GROUP_W, SSD_STATE) for bb in range(_SS_BB)]
        offs = [jnp.dot(h0.astype(BF16), cm_g, preferred_element_type=F32) for h0 in h0s]
        adds = [lax.dot_general(jnp.where(seq == bb, xw_g, 0.0).astype(BF16), bm_g, _NT, preferred_element_type=F32)
                for bb in range(_SS_BB)]
        yo = jnp.zeros((_GROUP_W, LANES), F32)
        for bb in range(_SS_BB):
            yo = jnp.where(seq == bb, offs[bb], yo)
            new = cd_g[:, n_tok * bb:n_tok * bb + 1] * h0s[bb] + adds[bb]
            ssm_ref[bb, hs] = new.reshape(SSD_HEADS_PER_GROUP, SSD_HEAD_DIM, SSD_STATE)
        yoffs.append(yo)

    y = ydiag + jnp.concatenate(yoffs, axis=0) * eac_e + ycol_ref[:, 0:1] * xs
    y = y * (zt * _sigmoid(zt))
    y = y * lax.rsqrt(jnp.mean(y * y, axis=0, keepdims=True) + RMS_EPS) * ycol_ref[:, 1:2]
    y_ref[...] = y.T[0:group, :]


def _ssd_sample(zxt, st_tok, dtt, ssm0, ccol, hcol, ycol, n_tok):
    b = ssm0.shape[0]
    m = b * n_tok
    group = n_tok * _SS_BB
    per_tile = LANES // group
    ssm_spec = pl.BlockSpec((_SS_BB, SSD_HEADS, SSD_HEAD_DIM, SSD_STATE), lambda i: (i, 0, 0, 0))
    return pl.pallas_call(
        functools.partial(_ssd_sample_kernel, n_tok),
        grid=(b // _SS_BB,),
        in_specs=[
            pl.BlockSpec((D_INNER + CONV_DIM, LANES), lambda i: (0, i // per_tile)),
            pl.BlockSpec((LANES, CONV_DIM), lambda i: (i // per_tile, 0)),
            pl.BlockSpec((_HEAD_ROWS, LANES), lambda i: (0, i // per_tile)),
            ssm_spec,
            _const_spec(ccol.shape), _const_spec(hcol.shape), _const_spec(ycol.shape),
        ],
        out_specs=[pl.BlockSpec((group, D_INNER), lambda i: (i, 0)), ssm_spec],
        out_shape=[jax.ShapeDtypeStruct((m, D_INNER), F32), jax.ShapeDtypeStruct(ssm0.shape, F32)],
        compiler_params=pltpu.CompilerParams(dimension_semantics=("parallel",)),
        name="ssd_sample",
    )(zxt, st_tok, dtt, ssm0, ccol, hcol, ycol)


def _load_tokens(ref, scr, dil):
    if dil == 1:
        return ref[0]
    n = ref.shape[1]
    halves = range(D_BRANCH // LANES)
    for r in range(dil):
        for half in halves:
            c0 = r * D_BRANCH + half * LANES
            scr[half, pl.ds(r, n, stride=dil), :] = ref[0, :, c0:c0 + LANES]
    return jnp.concatenate([scr[half] for half in halves], axis=1)


def _ffn_tail(h, gffn_ref, wg_ref, wu_ref, wd_ref, gfin_ref, y_ref, act_ref, final):
    hn = _rms(h, gffn_ref[...]).astype(BF16)
    for c in range(0, D_FF, FF_CHUNK):
        gate = jnp.dot(hn, wg_ref[:, c:c + FF_CHUNK], preferred_element_type=F32)
        up = jnp.dot(hn, wu_ref[:, c:c + FF_CHUNK], preferred_element_type=F32)
        act_ref[:, c:c + FF_CHUNK] = (gate * _sigmoid(gate) * up).astype(BF16)
    y = h + jnp.dot(act_ref[...], wd_ref[...], preferred_element_type=F32)
    y_ref[...] = _rms(y, gfin_ref[...]) if final else y


def _out_ffn_prompt_kernel(final, x_ref, *refs):
    att_refs = refs[:2 * N_BRANCH]
    ssd_ref, wo_ref, gffn_ref, wg_ref, wu_ref, wd_ref, gfin_ref, y_ref, act_ref = refs[2 * N_BRANCH:-2 * N_BRANCH]
    scrs = refs[-2 * N_BRANCH:]
    os_, ls = [], []
    for g, (win, dil) in enumerate(ATT_BRANCHES):
        os_.append(_load_tokens(att_refs[2 * g], scrs[2 * g], dil))
        ls.append(_load_tokens(att_refs[2 * g + 1], scrs[2 * g + 1], dil))
    mx = jnp.maximum(jnp.maximum(ls[0], ls[1]), ls[2])
    es = [jnp.exp(l - mx) for l in ls]
    inv = 1.0 / (es[0] + es[1] + es[2])
    h = x_ref[...]
    for g in range(N_BRANCH):
        a = (os_[g] * (es[g] * inv)).astype(BF16)
        h = h + jnp.dot(a, wo_ref[g * D_BRANCH:(g + 1) * D_BRANCH, :], preferred_element_type=F32)
    h = h + jnp.dot(ssd_ref[...].astype(BF16), wo_ref[D_ATT:D_MIX, :], preferred_element_type=F32)
    _ffn_tail(h, gffn_ref, wg_ref, wu_ref, wd_ref, gfin_ref, y_ref, act_ref, final)


def _out_ffn_sample_kernel(final, x_ref, att_ref, ssd_ref, wo_ref, gffn_ref, wg_ref, wu_ref, wd_ref, gfin_ref,
                           y_ref, act_ref):
    h = x_ref[...]
    h = h + jnp.dot(att_ref[...].astype(BF16), wo_ref[0:D_ATT, :], preferred_element_type=F32)
    h = h + jnp.dot(ssd_ref[...].astype(BF16), wo_ref[D_ATT:D_MIX, :], preferred_element_type=F32)
    _ffn_tail(h, gffn_ref, wg_ref, wu_ref, wd_ref, gfin_ref, y_ref, act_ref, final)


def _ffn_weight_specs():
    half = lambda j: pl.BlockSpec((D_MODEL, D_FF), lambda *_: (0, j), pipeline_mode=pl.Buffered(1))
    return [_const_spec((D_MIX, D_MODEL)), _const_spec((1, D_MODEL)), half(0), half(1),
            _const_spec((D_FF, D_MODEL)), _const_spec((1, D_MODEL))]


def _out_ffn_prompt(x2d, b, t, att, ssd2d, weights, final):
    m = b * t
    tm = FFN_TM
    tpb = t // tm
    tok = lambda w: pl.BlockSpec((tm, w), lambda i: (i, 0))
    att_specs = []
    for win, dil in ATT_BRANCHES:
        att_specs += [pl.BlockSpec((1, tm // dil, dil * D_BRANCH), lambda i: (i // tpb, i % tpb, 0))] * 2
    return pl.pallas_call(
        functools.partial(_out_ffn_prompt_kernel, final),
        grid=(m // tm,),
        in_specs=[tok(D_MODEL)] + att_specs + [tok(D_INNER)] + _ffn_weight_specs(),
        out_specs=tok(D_MODEL),
        out_shape=jax.ShapeDtypeStruct((m, D_MODEL), F32),
        scratch_shapes=[pltpu.VMEM((tm, D_FF), BF16)]
        + [pltpu.VMEM((D_BRANCH // LANES, tm, LANES), F32)] * (2 * N_BRANCH),
        compiler_params=pltpu.CompilerParams(dimension_semantics=("parallel",)),
        name="out_ffn_prompt",
    )(x2d, *att, ssd2d, *weights)


def _out_ffn_sample(x2d, att2d, ssd2d, weights, final):
    m = x2d.shape[0]
    tm = m // 2
    tok = lambda w: pl.BlockSpec((tm, w), lambda i: (i, 0))
    return pl.pallas_call(
        functools.partial(_out_ffn_sample_kernel, final),
        grid=(m // tm,),
        in_specs=[tok(D_MODEL), tok(D_ATT), tok(D_INNER)] + _ffn_weight_specs(),
        out_specs=tok(D_MODEL),
        out_shape=jax.ShapeDtypeStruct((m, D_MODEL), F32),
        scratch_shapes=[pltpu.VMEM((tm, D_FF), BF16)],
        compiler_params=pltpu.CompilerParams(dimension_semantics=("parallel",)),
        name="out_ffn_sample",
    )(x2d, att2d, ssd2d, *weights)


def _pad_lanes(v, width):
    return jnp.pad(v, ((0, 0), (0, width - v.shape[1])))


def kernel(x_prompt, x_sample, cache_kv_d1, cache_kv_d4, cache_kv_d16, state_conv, state_ssm, norm_mix, w_in, conv_w, conv_b, dt_bias, a_log, d_skip, norm_ssd, w_out, norm_ffn, w_gate_up, w_down, norm_final):
    bp, tp, _ = x_prompt.shape
    bs, ts, _ = x_sample.shape
    depth = w_in.shape[0]
    caches = (cache_kv_d1, cache_kv_d4, cache_kv_d16)
    hp, hs = x_prompt.reshape(bp * tp, D_MODEL), x_sample.reshape(bs * ts, D_MODEL)
    gfin = norm_final.reshape(1, D_MODEL)
    p_new, s_new = [[] for _ in range(5)], [[] for _ in range(5)]

    for l in range(depth):
        wi = w_in[l]
        wk, wv = wi[:, D_ATT:2 * D_ATT], wi[:, 2 * D_ATT:3 * D_ATT]
        wq = wi[:, :D_ATT].astype(BF16)
        wkv = jnp.concatenate([w[:, g * D_BRANCH:(g + 1) * D_BRANCH] for g in range(N_BRANCH) for w in (wk, wv)],
                              axis=1).astype(BF16)
        o = 3 * D_ATT
        wzx = wi[:, o:o + D_INNER + CONV_DIM].astype(BF16)
        wdt = _pad_lanes(wi[:, o + D_INNER + CONV_DIM:], DT_PAD).astype(BF16)
        gmix = norm_mix[l].reshape(1, D_MODEL)
        ssd_params = (conv_w[l], conv_b[l].reshape(1, CONV_DIM),
                      _pad_lanes(dt_bias[l].reshape(1, SSD_HEADS), DT_PAD),
                      _pad_lanes(a_log[l].reshape(1, SSD_HEADS), DT_PAD),
                      jnp.repeat(d_skip[l], SSD_HEAD_DIM).reshape(1, D_INNER),
                      norm_ssd[l].reshape(1, D_INNER))
        wgu = w_gate_up[l].astype(BF16)
        ffn_w = (w_out[l].astype(BF16), norm_ffn[l].reshape(1, D_MODEL), wgu, wgu, w_down[l].astype(BF16), gfin)
        final = l == depth - 1

        qkv, kts, (z, xbc, dt) = _in_proj_prompt(hp, bp, tp, gmix, wq, wkv, wzx, wdt)
        for g, (win, dil) in enumerate(ATT_BRANCHES):
            kt = kts[g].reshape(bp, 2, HEADS_PER_BRANCH, HEAD_DIM, win)
            p_new[g].append(jnp.transpose(kt, (0, 4, 1, 2, 3)))
        assert ts >= CONV_WIDTH - 1 and LANES % (ts * _SS_BB) == 0 and bs % _SS_BB == 0
        q_s, xbc_s, new_t, zxt, dtt = _in_proj_sample(hs, gmix, wq, wzx, wkv.T, wzx.T, wdt.T)
        att = _attn_prompt(qkv, bp, tp)

        xbc3 = xbc.reshape(bp, tp, CONV_DIM)
        conv0 = jnp.zeros((bp, SUBLANES, CONV_DIM), F32)
        ssm0 = jnp.zeros((bp, SSD_HEADS, SSD_HEAD_DIM, SSD_STATE), F32)
        caches_t = [jnp.transpose(c[l], (0, 2, 3, 4, 1)).reshape(bs, KV_W, c.shape[2]) for c in caches]
        y_ssd, ssm_p, att_s, shifted = _ssd_and_sample_attn(
            xbc3, z.reshape(bp, tp, D_INNER), dt.reshape(bp, tp, DT_PAD), conv0, ssm0, ssd_params,
            q_s.reshape(bs, ts, D_ATT), new_t, caches_t)
        p_new[3].append(xbc3[:, tp - (CONV_WIDTH - 1):])
        p_new[4].append(ssm_p)
        for g in range(N_BRANCH):
            lg = shifted[g].shape[2]
            s_new[g].append(jnp.transpose(shifted[g].reshape(bs, 2, HEADS_PER_BRANCH, HEAD_DIM, lg), (0, 4, 1, 2, 3)))
        hp = _out_ffn_prompt(hp, bp, tp, att, y_ssd.reshape(bp * tp, D_INNER), ffn_w, final)

        xbc3 = xbc_s.reshape(bs, ts, CONV_DIM)
        st_tok = jnp.pad(state_conv[l], ((0, 0), (0, ts - (CONV_WIDTH - 1)), (0, 0))).reshape(bs * ts, CONV_DIM)
        pad_heads = lambda v: jnp.pad(v, (0, _HEAD_ROWS - SSD_HEADS))
        ccol = jnp.broadcast_to(jnp.concatenate([conv_w[l], conv_b[l][None]], axis=0)[:, :, None],
                                (CONV_WIDTH + 1, CONV_DIM, LANES))
        hcol = jnp.stack([pad_heads(dt_bias[l]), pad_heads(a_log[l])], axis=1)
        ycol = jnp.stack([jnp.repeat(d_skip[l], SSD_HEAD_DIM), norm_ssd[l]], axis=1)
        y_ssd, ssm_s = _ssd_sample(zxt, st_tok, dtt, state_ssm[l], ccol, hcol, ycol, ts)
        s_new[3].append(jnp.concatenate([state_conv[l], xbc3], axis=1)[:, -(CONV_WIDTH - 1):])
        s_new[4].append(ssm_s)
        hs = _out_ffn_sample(hs, att_s.reshape(bs * ts, D_ATT), y_ssd, ffn_w, final)

    y_prompt = hp.reshape(bp, tp, D_MODEL)
    y_sample = hs.reshape(bs, ts, D_MODEL)
    p_out = [jnp.stack(a) for a in p_new]
    s_out = [jnp.stack(a) for a in s_new]
    return (y_prompt, y_sample, *p_out, *s_out)
```

```python
import functools

import jax
import jax.numpy as jnp
from jax import lax
from jax.experimental import pallas as pl
from jax.experimental.pallas import tpu as pltpu

F32 = jnp.float32
BF16 = jnp.bfloat16

D_MODEL = 1024
HEAD_DIM = 64
ATT_BRANCHES = ((128, 1), (512, 4), (2048, 16))
N_BRANCH = 3
HEADS_PER_BRANCH = 4
D_BRANCH = HEADS_PER_BRANCH * HEAD_DIM
D_ATT = N_BRANCH * D_BRANCH
SSD_HEADS = 12
SSD_HEAD_DIM = 64
D_INNER = SSD_HEADS * SSD_HEAD_DIM
SSD_GROUPS = 4
SSD_HEADS_PER_GROUP = 3
SSD_STATE = 128
CONV_WIDTH = 4
CONV_DIM = D_INNER + 2 * SSD_GROUPS * SSD_STATE
SSD_CHUNK = 128
D_MIX = D_ATT + D_INNER
D_FF = 2816
RMS_EPS = 1e-5
NEG_INF = -1e30
ATT_SCALE = HEAD_DIM ** -0.5
BAND = 128

LANES = 128
SUBLANES = 8
KV_W = 2 * D_BRANCH
DT_PAD = LANES
FF_CHUNK = 256
PROJ_CHUNK = 256
PROJ_TM = 512
ATT_QB = 2 * BAND
FFN_TM = 512

_NT = (((1,), (1,)), ((), ()))


def _rms(x, g):
    inv = lax.rsqrt(jnp.mean(x * x, axis=-1, keepdims=True) + RMS_EPS)
    return x * inv * g


def _sigmoid(x):
    return 1.0 / (1.0 + jnp.exp(-x))


def _softplus(x):
    return jnp.maximum(x, 0.0) + jnp.log1p(jnp.exp(-jnp.abs(x)))


def _const_spec(shape):
    nd = len(shape)
    return pl.BlockSpec(shape, lambda *_: (0,) * nd, pipeline_mode=pl.Buffered(1))


def _proj_store(xn, w_ref, out_ref, width):
    for c in range(0, width, PROJ_CHUNK):
        w = min(PROJ_CHUNK, width - c)
        out_ref[:, c:c + w] = jnp.dot(xn, w_ref[:, c:c + w], preferred_element_type=F32)


_Q_TILES = D_ATT // LANES
_KV_TILES = KV_W // LANES
_HALVES = D_BRANCH // LANES


def _in_proj_prompt_kernel(tiles_per_batch, x_ref, g_ref, wq_ref, wkv_ref, wzx_ref, wdt_ref, *refs):
    qkv_refs = refs[:3 * N_BRANCH]
    kt_refs = refs[3 * N_BRANCH:4 * N_BRANCH]
    z_ref, xbc_ref, dt_ref, acc = refs[4 * N_BRANCH:]
    tm = x_ref.shape[0]
    ti = lax.rem(pl.program_id(0), tiles_per_batch)
    xn = _rms(x_ref[...], g_ref[...]).astype(BF16)

    for c in range(0, D_ATT, PROJ_CHUNK):
        res = jnp.dot(xn, wq_ref[:, c:c + PROJ_CHUNK], preferred_element_type=F32) * ATT_SCALE
        for j in range(PROJ_CHUNK // LANES):
            acc[c // LANES + j] = res[:, j * LANES:(j + 1) * LANES]
    for c in range(0, N_BRANCH * KV_W, PROJ_CHUNK):
        res = jnp.dot(xn, wkv_ref[:, c:c + PROJ_CHUNK], preferred_element_type=F32)
        for j in range(PROJ_CHUNK // LANES):
            acc[_Q_TILES + c // LANES + j] = res[:, j * LANES:(j + 1) * LANES]
    _proj_store(xn, wzx_ref.at[:, 0:D_INNER], z_ref, D_INNER)
    _proj_store(xn, wzx_ref.at[:, D_INNER:D_INNER + CONV_DIM], xbc_ref, CONV_DIM)
    _proj_store(xn, wdt_ref, dt_ref, DT_PAD)

    for g, (win, dil) in enumerate(ATT_BRANCHES):
        n = tm // dil
        tiles = (_HALVES * g, _Q_TILES + _KV_TILES * g, _Q_TILES + _KV_TILES * g + _HALVES)
        for out_ref, t0 in zip(qkv_refs[3 * g:3 * g + 3], tiles):
            for r in range(dil):
                for half in range(_HALVES):
                    c0 = r * D_BRANCH + half * LANES
                    rows = acc[t0 + half] if dil == 1 else acc[t0 + half, pl.ds(r, n, stride=dil), :]
                    out_ref[0, :, c0:c0 + LANES] = rows.astype(BF16)

    for g, (win, dil) in enumerate(ATT_BRANCHES):
        w = kt_refs[g].shape[2]
        first = tiles_per_batch - win // w

        @pl.when(ti >= first)
        def _(g=g, w=w):
            for j in range(_KV_TILES):
                kt_refs[g][0, j * LANES:(j + 1) * LANES, :] = acc[_Q_TILES + _KV_TILES * g + j, tm - w:tm, :].T


def _in_proj_prompt(x2d, b, t, g, wq, wkv, wzx, wdt):
    m = b * t
    tm = PROJ_TM
    tpb = t // tm
    tok = lambda w: pl.BlockSpec((tm, w), lambda i: (i, 0))
    qkv_specs, qkv_shapes, kt_specs, kt_shapes = [], [], [], []
    for win, dil in ATT_BRANCHES:
        spec = pl.BlockSpec((1, tm // dil, dil * D_BRANCH), lambda i: (i // tpb, i % tpb, 0))
        shape = jax.ShapeDtypeStruct((b, t // dil, dil * D_BRANCH), BF16)
        qkv_specs += [spec] * 3
        qkv_shapes += [shape] * 3
        w = min(win, tm)
        first = tpb - win // w
        kt_specs.append(pl.BlockSpec((1, KV_W, w), lambda i, first=first: (i // tpb, 0, jnp.maximum(i % tpb - first, 0))))
        kt_shapes.append(jax.ShapeDtypeStruct((b, KV_W, win), F32))
    outs = pl.pallas_call(
        functools.partial(_in_proj_prompt_kernel, tpb),
        grid=(m // tm,),
        in_specs=[tok(D_MODEL), _const_spec((1, D_MODEL)), _const_spec((D_MODEL, D_ATT)),
                  _const_spec((D_MODEL, N_BRANCH * KV_W)), _const_spec((D_MODEL, D_INNER + CONV_DIM)),
                  _const_spec((D_MODEL, DT_PAD))],
        out_specs=qkv_specs + kt_specs + [tok(D_INNER), tok(CONV_DIM), tok(DT_PAD)],
        out_shape=qkv_shapes + kt_shapes + [jax.ShapeDtypeStruct((m, w), F32) for w in (D_INNER, CONV_DIM, DT_PAD)],
        scratch_shapes=[pltpu.VMEM((_Q_TILES + N_BRANCH * _KV_TILES, tm, LANES), F32)],
        compiler_params=pltpu.CompilerParams(dimension_semantics=("arbitrary",)),
        name="in_proj_prompt",
    )(x2d, g, wq, wkv, wzx, wdt)
    n = 3 * N_BRANCH
    return outs[:n], outs[n:n + N_BRANCH], outs[n + N_BRANCH:]


def _in_proj_sample_kernel(x_ref, g_ref, wq_ref, wzx_ref, wkvt_ref, wzxt_ref, wdtt_ref,
                           q_ref, xbc_ref, kvt_ref, zxt_ref, dtt_ref):
    xn = _rms(x_ref[...], g_ref[...]).astype(BF16)
    for c in range(0, D_ATT, PROJ_CHUNK):
        q = jnp.dot(xn, wq_ref[:, c:c + PROJ_CHUNK], preferred_element_type=F32) * ATT_SCALE
        q_ref[:, c:c + PROJ_CHUNK] = q.astype(BF16).astype(F32)
    _proj_store(xn, wzx_ref.at[:, D_INNER:D_INNER + CONV_DIM], xbc_ref, CONV_DIM)
    for wt_ref, out_ref in ((wkvt_ref, kvt_ref), (wzxt_ref, zxt_ref), (wdtt_ref, dtt_ref)):
        for r in range(0, wt_ref.shape[0], PROJ_CHUNK):
            n = min(PROJ_CHUNK, wt_ref.shape[0] - r)
            out_ref[r:r + n, :] = lax.dot_general(wt_ref[r:r + n, :], xn, _NT, preferred_element_type=F32)


def _in_proj_sample(x2d, g, wq, wzx, wkvt, wzxt, wdtt):
    m = x2d.shape[0]
    ins = (x2d, g, wq, wzx, wkvt, wzxt, wdtt)
    shapes = [(m, D_ATT), (m, CONV_DIM), (wkvt.shape[0], m), (wzxt.shape[0], m), (wdtt.shape[0], m)]
    return pl.pallas_call(
        _in_proj_sample_kernel,
        grid=(1,),
        in_specs=[_const_spec(a.shape) for a in ins],
        out_specs=[pl.BlockSpec(s, lambda i: (0, 0)) for s in shapes],
        out_shape=[jax.ShapeDtypeStruct(s, F32) for s in shapes],
        name="in_proj_sample",
    )(*ins)


_ATT_STEPS = 16
_ATT_ROWS = 128


def _attn_prompt_kernel(*refs):
    ins, outs = refs[:5 * N_BRANCH], refs[5 * N_BRANCH:]
    j = pl.program_id(1)
    qi = lax.broadcasted_iota(jnp.int32, (BAND, 2 * BAND), 0)
    kj = lax.broadcasted_iota(jnp.int32, (BAND, 2 * BAND), 1)
    dist = BAND + qi - kj
    in_band = (dist >= 0) & (dist <= BAND)

    tasks = []
    for g, (win, dil) in enumerate(ATT_BRANCHES):
        q_ref, kp_ref, kc_ref, vp_ref, vc_ref = ins[5 * g:5 * g + 5]
        i = lax.rem(j, _ATT_STEPS // dil)
        kc, vc = kc_ref[0], vc_ref[0]
        for sb in range(ATT_QB // BAND):
            rows = slice(sb * BAND, (sb + 1) * BAND)
            if sb == 0:
                k = jnp.concatenate([kp_ref[0], kc[rows]], axis=0)
                v = jnp.concatenate([vp_ref[0], vc[rows]], axis=0)
                valid = in_band & ((kj >= BAND) | (i > 0))
            else:
                k = kc[(sb - 1) * BAND:(sb + 1) * BAND]
                v = vc[(sb - 1) * BAND:(sb + 1) * BAND]
                valid = in_band
            for h in range(HEADS_PER_BRANCH):
                sl = slice(h * HEAD_DIM, (h + 1) * HEAD_DIM)
                for r0 in range(0, BAND, _ATT_ROWS):
                    sub = slice(sb * BAND + r0, sb * BAND + r0 + _ATT_ROWS)
                    tasks.append((q_ref[0, sub, sl], k[:, sl], v[:, sl], valid[r0:r0 + _ATT_ROWS],
                                  outs[2 * g], outs[2 * g + 1], sub, sl))

    ss = [jnp.where(t[3], lax.dot_general(t[0], t[1], _NT, preferred_element_type=F32), NEG_INF) for t in tasks]
    ms = [jnp.max(s, axis=-1, keepdims=True) for s in ss]
    ps = [jnp.exp(s - m).astype(BF16) for s, m in zip(ss, ms)]
    ones = jnp.ones((2 * BAND, HEAD_DIM), BF16)
    dens = [jnp.dot(p, ones, preferred_element_type=F32) for p in ps]
    nums = [jnp.dot(p, t[2], preferred_element_type=F32) for p, t in zip(ps, tasks)]
    for t, m, den, num in zip(tasks, ms, dens, nums):
        o_ref, l_ref, rows, sl = t[4:]
        o_ref[0, rows, sl] = num / den
        l_ref[0, rows, sl] = m + jnp.log(den)


def _attn_prompt(qkv, b, t):
    assert t // ATT_QB == _ATT_STEPS
    in_specs, args, out_specs, out_shapes = [], [], [], []
    for g, (win, dil) in enumerate(ATT_BRANCHES):
        nblk = _ATT_STEPS // dil
        cur = pl.BlockSpec((1, ATT_QB, D_BRANCH), lambda bi, j, nblk=nblk: (bi, j % nblk, j // nblk))
        prv = pl.BlockSpec((1, BAND, D_BRANCH),
                           lambda bi, j, nblk=nblk: (bi, jnp.maximum((ATT_QB // BAND) * (j % nblk) - 1, 0), j // nblk))
        q, k, v = qkv[3 * g:3 * g + 3]
        in_specs += [cur, prv, cur, prv, cur]
        args += [q, k, k, v, v]
        out_specs += [cur, cur]
        out_shapes += [jax.ShapeDtypeStruct((b, t // dil, dil * D_BRANCH), F32)] * 2
    return pl.pallas_call(
        _attn_prompt_kernel,
        grid=(b, _ATT_STEPS),
        in_specs=in_specs,
        out_specs=out_specs,
        out_shape=out_shapes,
        compiler_params=pltpu.CompilerParams(dimension_semantics=("parallel", "arbitrary")),
        name="attn_prompt",
    )(*args)


_QROWS = 4 * SUBLANES
_SHIFT_ROWS = 64


def _sample_attn_body(n_new, b, q_ref, nt_ref, c1_ref, c4_ref, c16_ref,
                      att_ref, o1_ref, o4_ref, o16_ref, nbuf):
    per_tile = LANES // n_new
    new0 = LANES - n_new
    shift = new0 - n_new * lax.rem(b, per_tile)
    nbuf[...] = pltpu.roll(nt_ref[...], shift, axis=1)

    q = q_ref[0]
    row = lax.broadcasted_iota(jnp.int32, (_QROWS, D_BRANCH), 0)
    lane = lax.broadcasted_iota(jnp.int32, (_QROWS, D_BRANCH), 1)
    hmask = (lane // HEAD_DIM) == (row % SUBLANES)

    branches = range(N_BRANCH)
    c_refs = (c1_ref, c4_ref, c16_ref)
    dils = [dil for _, dil in ATT_BRANCHES]
    qbds = []
    for g in branches:
        qs = q[:, g * D_BRANCH:(g + 1) * D_BRANCH]
        qb = jnp.concatenate([jnp.broadcast_to(qs[s:s + 1], (SUBLANES, D_BRANCH)) for s in range(n_new)], axis=0)
        qbds.append(jnp.where(hmask, qb, 0.0).astype(BF16))
    kts = [c_refs[g][0, 0:D_BRANCH, :].astype(BF16) for g in branches]
    kns = [nbuf[g * KV_W:g * KV_W + D_BRANCH, :].astype(BF16) for g in branches]
    scs = [jnp.dot(qbds[g], kts[g], preferred_element_type=F32) for g in branches]
    scns = [jnp.dot(qbds[g], kns[g], preferred_element_type=F32) for g in branches]
    vts = [c_refs[g][0, D_BRANCH:KV_W, :].astype(BF16) for g in branches]
    vns = [nbuf[g * KV_W + D_BRANCH:(g + 1) * KV_W, :].astype(BF16) for g in branches]
    s_n = lax.broadcasted_iota(jnp.int32, (_QROWS, LANES), 0) // SUBLANES
    j_n = lax.broadcasted_iota(jnp.int32, (_QROWS, LANES), 1) - new0
    for g in branches:
        length = c_refs[g].shape[2]
        s_c = lax.broadcasted_iota(jnp.int32, (_QROWS, length), 0) // SUBLANES
        l_c = lax.broadcasted_iota(jnp.int32, (_QROWS, length), 1)
        valid_c = (l_c >= s_c) & (((l_c - s_c) & (dils[g] - 1)) == 0)
        valid_n = (j_n >= 0) & (j_n <= s_n) & (((s_n - j_n) & (dils[g] - 1)) == 0)
        scs[g] = jnp.where(valid_c, scs[g], NEG_INF)
        scns[g] = jnp.where(valid_n, scns[g], NEG_INF)
    ms = [jnp.maximum(jnp.max(scs[g], axis=-1, keepdims=True), jnp.max(scns[g], axis=-1, keepdims=True))
          for g in branches]
    ps = [jnp.exp(scs[g] - ms[g]) for g in branches]
    pns = [jnp.exp(scns[g] - ms[g]) for g in branches]
    dens = [jnp.sum(ps[g], axis=-1, keepdims=True) + jnp.sum(pns[g], axis=-1, keepdims=True) for g in branches]
    us = [lax.dot_general(ps[g].astype(BF16), vts[g], _NT, preferred_element_type=F32)
          + lax.dot_general(pns[g].astype(BF16), vns[g], _NT, preferred_element_type=F32) for g in branches]
    lses = [ms[g] + jnp.log(dens[g]) for g in branches]

    mx = jnp.maximum(jnp.maximum(lses[0], lses[1]), lses[2])
    es = [jnp.exp(l - mx) for l in lses]
    tot = es[0] + es[1] + es[2]
    for g in range(N_BRANCH):
        coef = es[g] / tot / dens[g]
        a = jnp.where(hmask, us[g] * coef, 0.0)
        a = a.reshape(n_new, SUBLANES, D_BRANCH).sum(axis=1)
        att_ref[0, :, g * D_BRANCH:(g + 1) * D_BRANCH] = a

    def shift_slab(g, c_ref, o_ref, r0):
        keep = lax.broadcasted_iota(jnp.int32, (_SHIFT_ROWS, LANES), 1) < new0
        ntiles = c_ref.shape[2] // LANES
        rows = slice(r0, r0 + _SHIFT_ROWS)
        nxt = pltpu.roll(c_ref[0, rows, 0:LANES], new0, axis=1)
        for j in range(ntiles):
            cur = nxt
            if j + 1 < ntiles:
                nxt = pltpu.roll(c_ref[0, rows, (j + 1) * LANES:(j + 2) * LANES], new0, axis=1)
            else:
                nxt = nbuf[g * KV_W + r0:g * KV_W + r0 + _SHIFT_ROWS, :]
            o_ref[0, rows, j * LANES:(j + 1) * LANES] = jnp.where(keep, cur, nxt)

    return [functools.partial(shift_slab, g, c_ref, o_ref, r0)
            for r0 in range(0, KV_W, _SHIFT_ROWS)
            for g, (c_ref, o_ref) in enumerate(zip((c1_ref, c4_ref, c16_ref), (o1_ref, o4_ref, o16_ref)))]


_GROUP_W = SSD_HEADS_PER_GROUP * SSD_HEAD_DIM
_STATE_W = 2 * LANES
_HEAD_BATCH = 4


def _split3(a):
    a1 = a.astype(BF16)
    r1 = a - a1.astype(F32)
    a2 = r1.astype(BF16)
    a3 = (r1 - a2.astype(F32)).astype(BF16)
    return jnp.concatenate([a1, a2, a3], axis=1)


def _ssd_kernel(xbc_ref, z_ref, dt_ref, conv0_ref, ssm0_ref, cw_ref, cb_ref,
                dtb_ref, alog_ref, dsk_ref, nrm_ref, y_ref, ssm_ref, tail, st, ybuf):
    c = pl.program_id(1)
    nc = pl.num_programs(1)
    cl = SSD_CHUNK

    @pl.when(c == 0)
    def _():
        tail[...] = conv0_ref[0]
        zpad = jnp.zeros((_STATE_W - _GROUP_W, SSD_STATE), F32)
        for g in range(SSD_GROUPS):
            s = jnp.concatenate([ssm0_ref[0, SSD_HEADS_PER_GROUP * g + r] for r in range(SSD_HEADS_PER_GROUP)]
                                + [zpad], axis=0)
            st[g] = s.T

    xcur = xbc_ref[0]
    prev = tail[...]
    row8 = lax.broadcasted_iota(jnp.int32, (SUBLANES, CONV_DIM), 0)
    xc = cb_ref[...] + cw_ref[CONV_WIDTH - 1:CONV_WIDTH, :] * xcur
    for s in range(1, CONV_WIDTH):
        sh = pltpu.roll(xcur, s, axis=0)
        head = jnp.where(row8 < s, pltpu.roll(prev, s, axis=0), sh[0:SUBLANES])
        sh = jnp.concatenate([head, sh[SUBLANES:]], axis=0)
        xc = xc + cw_ref[CONV_WIDTH - 1 - s:CONV_WIDTH - s, :] * sh
    tail[...] = xcur[cl - SUBLANES:cl]
    xc = xc * _sigmoid(xc)
    xs = xc[:, :D_INNER]
    bm = xc[:, D_INNER:D_INNER + SSD_GROUPS * SSD_STATE]
    cm = xc[:, D_INNER + SSD_GROUPS * SSD_STATE:]

    r_i = lax.broadcasted_iota(jnp.int32, (cl, cl), 0)
    c_i = lax.broadcasted_iota(jnp.int32, (cl, cl), 1)
    causal = r_i >= c_i
    dt = _softplus(dt_ref[0] + dtb_ref[...])
    da = dt * (-jnp.exp(alog_ref[...]))
    parts = jnp.dot(causal.astype(BF16), _split3(da), preferred_element_type=F32)
    acum = parts[:, :LANES] + parts[:, LANES:2 * LANES] + parts[:, 2 * LANES:]
    wdec = jnp.exp(acum[cl - 1:cl, :] - acum) * dt
    eac = jnp.exp(acum)
    e_r = lax.broadcasted_iota(jnp.int32, (3 * LANES, D_INNER), 0)
    e_c = lax.broadcasted_iota(jnp.int32, (3 * LANES, D_INNER), 1)
    expand = (e_c // SSD_HEAD_DIM == e_r % LANES).astype(BF16)
    both = jnp.dot(jnp.concatenate([_split3(wdec), _split3(eac)], axis=0), expand, preferred_element_type=F32)
    xw = xs * both[:cl]
    eac_e = both[cl:]
    cd_e = eac_e[cl - 1:cl, :]
    acum_t = acum.T
    dt_t = dt.T

    groups = range(SSD_GROUPS)
    nsl = [slice(g * SSD_STATE, (g + 1) * SSD_STATE) for g in groups]
    gsl = [slice(g * _GROUP_W, (g + 1) * _GROUP_W) for g in groups]
    bgs = [bm[:, nsl[g]] for g in groups]
    cgbs = [cm[:, nsl[g]].astype(BF16) for g in groups]
    sgs = [st[g] for g in groups]
    cbs = [lax.dot_general(cgbs[g], bgs[g].astype(BF16), _NT, preferred_element_type=F32) for g in groups]
    yoffs = [jnp.dot(cgbs[g], sgs[g].astype(BF16), preferred_element_type=F32)[:, :_GROUP_W] for g in groups]
    snews = [jnp.dot(bgs[g].T.astype(BF16), xw[:, gsl[g]].astype(BF16), preferred_element_type=F32)
             for g in groups]
    for g in groups:
        st[g, :, 0:_GROUP_W] = cd_e[:, gsl[g]] * sgs[g][:, :_GROUP_W] + snews[g]
    for h0 in range(0, SSD_HEADS, _HEAD_BATCH):
        hs = range(h0, h0 + _HEAD_BATCH)
        lmats = [jnp.exp(jnp.where(causal, acum[:, h:h + 1] - acum_t[h:h + 1, :], NEG_INF)) for h in hs]
        mms = [(cbs[h // SSD_HEADS_PER_GROUP] * lm * dt_t[h:h + 1, :]).astype(BF16) for h, lm in zip(hs, lmats)]
        hsl = [slice(h * SSD_HEAD_DIM, (h + 1) * SSD_HEAD_DIM) for h in hs]
        yds = [jnp.dot(mm, xs[:, sl].astype(BF16), preferred_element_type=F32) for mm, sl in zip(mms, hsl)]
        for sl, yd in zip(hsl, yds):
            ybuf[:, sl] = yd

    y = ybuf[...] + jnp.concatenate(yoffs, axis=1) * eac_e + dsk_ref[...] * xs
    z = z_ref[0]
    y = y * (z * _sigmoid(z))
    y_ref[0] = _rms(y, nrm_ref[...])

    @pl.when(c == nc - 1)
    def _():
        for g in range(SSD_GROUPS):
            t = st[g].T
            for r in range(SSD_HEADS_PER_GROUP):
                ssm_ref[0, SSD_HEADS_PER_GROUP * g + r] = t[r * SSD_HEAD_DIM:(r + 1) * SSD_HEAD_DIM, :]


_N_SSD_IN, _N_SA_IN, _N_SSD_OUT, _N_SA_OUT, _N_SSD_SCR = 11, 5, 2, 4, 3


def _ssd_and_sample_attn_kernel(n_new, *refs):
    edges = [0]
    for n in (_N_SSD_IN, _N_SA_IN, _N_SSD_OUT, _N_SA_OUT, _N_SSD_SCR, 1):
        edges.append(edges[-1] + n)
    ssd_in, sa_in, ssd_out, sa_out, ssd_scr, sa_scr = (refs[a:b] for a, b in zip(edges[:-1], edges[1:]))
    seq = pl.program_id(0) * pl.num_programs(1) + pl.program_id(1)
    for shift_slab in _sample_attn_body(n_new, seq, *sa_in, *sa_out, *sa_scr):
        shift_slab()
    _ssd_kernel(*ssd_in, *ssd_out, *ssd_scr)


def _ssd_and_sample_attn(xbc, z, dt, conv0, ssm0, params, q_s, new_t, caches_t):
    b, t, _ = xbc.shape
    cl = SSD_CHUNK
    nc = t // cl
    s, n_new, _ = q_s.shape
    assert s == b * nc
    per_tile = LANES // n_new
    cw, cb, dtb, alog, dsk, nrm = params
    tok = lambda w: pl.BlockSpec((1, cl, w), lambda bi, ci: (bi, ci, 0))
    ssm_spec = pl.BlockSpec((1, SSD_HEADS, SSD_HEAD_DIM, SSD_STATE), lambda bi, ci: (bi, 0, 0, 0))
    seq_spec = lambda shape: pl.BlockSpec(shape, lambda bi, ci: (bi * nc + ci, 0, 0))
    cache_specs = [seq_spec((1, KV_W, c.shape[2])) for c in caches_t]
    outs = pl.pallas_call(
        functools.partial(_ssd_and_sample_attn_kernel, n_new),
        grid=(b, nc),
        in_specs=[
            tok(CONV_DIM), tok(D_INNER), tok(DT_PAD),
            pl.BlockSpec((1, SUBLANES, CONV_DIM), lambda bi, ci: (bi, 0, 0)),
            ssm_spec,
            _const_spec((CONV_WIDTH, CONV_DIM)), _const_spec((1, CONV_DIM)),
            _const_spec((1, DT_PAD)), _const_spec((1, DT_PAD)),
            _const_spec((1, D_INNER)), _const_spec((1, D_INNER)),
            seq_spec((1, n_new, D_ATT)),
            pl.BlockSpec((N_BRANCH * KV_W, LANES), lambda bi, ci: (0, (bi * nc + ci) // per_tile)),
        ] + cache_specs,
        out_specs=[tok(D_INNER), ssm_spec, seq_spec((1, n_new, D_ATT))] + cache_specs,
        out_shape=[jax.ShapeDtypeStruct((b, t, D_INNER), F32),
                   jax.ShapeDtypeStruct((b, SSD_HEADS, SSD_HEAD_DIM, SSD_STATE), F32),
                   jax.ShapeDtypeStruct((s, n_new, D_ATT), F32)]
        + [jax.ShapeDtypeStruct(c.shape, F32) for c in caches_t],
        scratch_shapes=[pltpu.VMEM((SUBLANES, CONV_DIM), F32),
                        pltpu.VMEM((SSD_GROUPS, SSD_STATE, _STATE_W), F32),
                        pltpu.VMEM((cl, D_INNER), F32),
                        pltpu.VMEM((N_BRANCH * KV_W, LANES), F32)],
        compiler_params=pltpu.CompilerParams(dimension_semantics=("parallel", "arbitrary")),
        name="ssd_and_sample_attn",
    )(xbc, z, dt, conv0, ssm0, cw, cb, dtb, alog, dsk, nrm, q_s, new_t, *caches_t)
    return outs[0], outs[1], outs[2], outs[3:]


_SS_BB = 16
_HEAD_ROWS = 2 * SUBLANES


def _heads_to_features(a):
    return jnp.concatenate([jnp.broadcast_to(a[h:h + 1], (SSD_HEAD_DIM, LANES)) for h in range(SSD_HEADS)], axis=0)


def _ssd_sample_kernel(n_tok, zxt_ref, st_ref, dtt_ref, ssm0_ref, ccol_ref, hcol_ref, ycol_ref, y_ref, ssm_ref):
    group = n_tok * _SS_BB
    sub = lax.rem(pl.program_id(0), LANES // group)
    shift = lax.rem(LANES - group * sub, LANES)
    roll = lambda a, s: pltpu.roll(a, s, axis=1)
    lane = lax.broadcasted_iota(jnp.int32, (1, LANES), 1)
    tok = lane % n_tok
    seq = lane // n_tok

    zx = roll(zxt_ref[...], shift)
    zt, xt = zx[:D_INNER], zx[D_INNER:]
    stt = roll(st_ref[...].T, shift)
    dtr = roll(dtt_ref[...], shift)

    xc = ccol_ref[CONV_WIDTH] + ccol_ref[CONV_WIDTH - 1] * xt
    for k in range(CONV_WIDTH - 1):
        back = CONV_WIDTH - 1 - k
        src = jnp.where(tok >= back, roll(xt, back), roll(stt, (LANES - k) % LANES))
        xc = xc + ccol_ref[k] * src
    xc = xc * _sigmoid(xc)
    gn = SSD_GROUPS * SSD_STATE
    xs = xc[:D_INNER]
    bmb = xc[D_INNER:D_INNER + gn].astype(BF16)
    cmb = xc[D_INNER + gn:].astype(BF16)
    bmf, cmf = bmb.astype(F32), cmb.astype(F32)

    dt = _softplus(dtr + hcol_ref[:, 0:1])
    da = dt * (-jnp.exp(hcol_ref[:, 1:2]))
    acum = da
    for s in range(1, n_tok):
        acum = acum + jnp.where(tok >= s, roll(da, s), 0.0)
    last = jnp.where(tok == n_tok - 1, acum, 0.0)
    tot = last
    for s in range(1, n_tok):
        tot = tot + roll(last, LANES - s)
    xw = xs * _heads_to_features(jnp.exp(tot - acum) * dt)
    eac_e = _heads_to_features(jnp.exp(acum))
    cd_e = _heads_to_features(jnp.exp(tot))

    row = lax.broadcasted_iota(jnp.int32, (_HEAD_ROWS, LANES), 0)
    ydiag = None
    for d in range(n_tok):
        prod = cmf * (bmf if d == 0 else roll(bmf, d))
        cb = jnp.zeros((_HEAD_ROWS, LANES), F32)
        for g in range(SSD_GROUPS):
            cbg = jnp.sum(prod[g * SSD_STATE:(g + 1) * SSD_STATE], axis=0, keepdims=True)
            in_group = (row >= SSD_HEADS_PER_GROUP * g) & (row < SSD_HEADS_PER_GROUP * (g + 1))
            cb = jnp.where(in_group, cbg, cb)
        if d == 0:
            coef, xsh = cb * dt, xs
        else:
            ok = tok >= d
            decay = jnp.exp(jnp.where(ok, acum - roll(acum, d), 0.0))
            coef, xsh = jnp.where(ok, cb * decay * roll(dt, d), 0.0), roll(xs, d)
        term = _heads_to_features(coef) * xsh
        ydiag = term if ydiag is None else ydiag + term

    yoffs = []
    for g in range(SSD_GROUPS):
        hs = slice(SSD_HEADS_PER_GROUP * g, SSD_HEADS_PER_GROUP * (g + 1))
        rs = slice(_GROUP_W * g, _GROUP_W * (g + 1))
        cm_g, bm_g = cmb[g * SSD_STATE:(g + 1) * SSD_STATE], bmb[g * SSD_STATE:(g + 1) * SSD_STATE]
        xw_g, cd_g = xw[rs], cd_e[rs]
        h0s = [ssm0_ref[bb, hs].reshape(_GROUP_W, SSD_STATE) for bb in range(_SS_BB)]
        offs = [jnp.dot(h0.astype(BF16), cm_g, preferred_element_type=F32) for h0 in h0s]
        adds = [lax.dot_general(jnp.where(seq == bb, xw_g, 0.0).astype(BF16), bm_g, _NT, preferred_element_type=F32)
                for bb in range(_SS_BB)]
        yo = jnp.zeros((_GROUP_W, LANES), F32)
        for bb in range(_SS_BB):
            yo = jnp.where(seq == bb, offs[bb], yo)
            new = cd_g[:, n_tok * bb:n_tok * bb + 1] * h0s[bb] + adds[bb]
            ssm_ref[bb, hs] = new.reshape(SSD_HEADS_PER_GROUP, SSD_HEAD_DIM, SSD_STATE)
        yoffs.append(yo)

    y = ydiag + jnp.concatenate(yoffs, axis=0) * eac_e + ycol_ref[:, 0:1] * xs
    y = y * (zt * _sigmoid(zt))
    y = y * lax.rsqrt(jnp.mean(y * y, axis=0, keepdims=True) + RMS_EPS) * ycol_ref[:, 1:2]
    y_ref[...] = y.T[0:group, :]


def _ssd_sample(zxt, st_tok, dtt, ssm0, ccol, hcol, ycol, n_tok):
    b = ssm0.shape[0]
    m = b * n_tok
    group = n_tok * _SS_BB
    per_tile = LANES // group
    ssm_spec = pl.BlockSpec((_SS_BB, SSD_HEADS, SSD_HEAD_DIM, SSD_STATE), lambda i: (i, 0, 0, 0))
    return pl.pallas_call(
        functools.partial(_ssd_sample_kernel, n_tok),
        grid=(b // _SS_BB,),
        in_specs=[
            pl.BlockSpec((D_INNER + CONV_DIM, LANES), lambda i: (0, i // per_tile)),
            pl.BlockSpec((LANES, CONV_DIM), lambda i: (i // per_tile, 0)),
            pl.BlockSpec((_HEAD_ROWS, LANES), lambda i: (0, i // per_tile)),
            ssm_spec,
            _const_spec(ccol.shape), _const_spec(hcol.shape), _const_spec(ycol.shape),
        ],
        out_specs=[pl.BlockSpec((group, D_INNER), lambda i: (i, 0)), ssm_spec],
        out_shape=[jax.ShapeDtypeStruct((m, D_INNER), F32), jax.ShapeDtypeStruct(ssm0.shape, F32)],
        compiler_params=pltpu.CompilerParams(dimension_semantics=("parallel",)),
        name="ssd_sample",
    )(zxt, st_tok, dtt, ssm0, ccol, hcol, ycol)


def _load_tokens(ref, scr, dil):
    if dil == 1:
        return ref[0]
    n = ref.shape[1]
    halves = range(D_BRANCH // LANES)
    for r in range(dil):
        for half in halves:
            c0 = r * D_BRANCH + half * LANES
            scr[half, pl.ds(r, n, stride=dil), :] = ref[0, :, c0:c0 + LANES]
    return jnp.concatenate([scr[half] for half in halves], axis=1)


def _ffn_tail(h, gffn_ref, wg_ref, wu_ref, wd_ref, gfin_ref, y_ref, act_ref, final):
    hn = _rms(h, gffn_ref[...]).astype(BF16)
    for c in range(0, D_FF, FF_CHUNK):
        gate = jnp.dot(hn, wg_ref[:, c:c + FF_CHUNK], preferred_element_type=F32)
        up = jnp.dot(hn, wu_ref[:, c:c + FF_CHUNK], preferred_element_type=F32)
        act_ref[:, c:c + FF_CHUNK] = (gate * _sigmoid(gate) * up).astype(BF16)
    y = h + jnp.dot(act_ref[...], wd_ref[...], preferred_element_type=F32)
    y_ref[...] = _rms(y, gfin_ref[...]) if final else y


def _out_ffn_prompt_kernel(final, x_ref, *refs):
    att_refs = refs[:2 * N_BRANCH]
    ssd_ref, wo_ref, gffn_ref, wg_ref, wu_ref, wd_ref, gfin_ref, y_ref, act_ref = refs[2 * N_BRANCH:-2 * N_BRANCH]
    scrs = refs[-2 * N_BRANCH:]
    os_, ls = [], []
    for g, (win, dil) in enumerate(ATT_BRANCHES):
        os_.append(_load_tokens(att_refs[2 * g], scrs[2 * g], dil))
        ls.append(_load_tokens(att_refs[2 * g + 1], scrs[2 * g + 1], dil))
    mx = jnp.maximum(jnp.maximum(ls[0], ls[1]), ls[2])
    es = [jnp.exp(l - mx) for l in ls]
    inv = 1.0 / (es[0] + es[1] + es[2])
    h = x_ref[...]
    for g in range(N_BRANCH):
        a = (os_[g] * (es[g] * inv)).astype(BF16)
        h = h + jnp.dot(a, wo_ref[g * D_BRANCH:(g + 1) * D_BRANCH, :], preferred_element_type=F32)
    h = h + jnp.dot(ssd_ref[...].astype(BF16), wo_ref[D_ATT:D_MIX, :], preferred_element_type=F32)
    _ffn_tail(h, gffn_ref, wg_ref, wu_ref, wd_ref, gfin_ref, y_ref, act_ref, final)


def _out_ffn_sample_kernel(final, x_ref, att_ref, ssd_ref, wo_ref, gffn_ref, wg_ref, wu_ref, wd_ref, gfin_ref,
                           y_ref, act_ref):
    h = x_ref[...]
    h = h + jnp.dot(att_ref[...].astype(BF16), wo_ref[0:D_ATT, :], preferred_element_type=F32)
    h = h + jnp.dot(ssd_ref[...].astype(BF16), wo_ref[D_ATT:D_MIX, :], preferred_element_type=F32)
    _ffn_tail(h, gffn_ref, wg_ref, wu_ref, wd_ref, gfin_ref, y_ref, act_ref, final)


def _ffn_weight_specs():
    half = lambda j: pl.BlockSpec((D_MODEL, D_FF), lambda *_: (0, j), pipeline_mode=pl.Buffered(1))
    return [_const_spec((D_MIX, D_MODEL)), _const_spec((1, D_MODEL)), half(0), half(1),
            _const_spec((D_FF, D_MODEL)), _const_spec((1, D_MODEL))]


def _out_ffn_prompt(x2d, b, t, att, ssd2d, weights, final):
    m = b * t
    tm = FFN_TM
    tpb = t // tm
    tok = lambda w: pl.BlockSpec((tm, w), lambda i: (i, 0))
    att_specs = []
    for win, dil in ATT_BRANCHES:
        att_specs += [pl.BlockSpec((1, tm // dil, dil * D_BRANCH), lambda i: (i // tpb, i % tpb, 0))] * 2
    return pl.pallas_call(
        functools.partial(_out_ffn_prompt_kernel, final),
        grid=(m // tm,),
        in_specs=[tok(D_MODEL)] + att_specs + [tok(D_INNER)] + _ffn_weight_specs(),
        out_specs=tok(D_MODEL),
        out_shape=jax.ShapeDtypeStruct((m, D_MODEL), F32),
        scratch_shapes=[pltpu.VMEM((tm, D_FF), BF16)]
        + [pltpu.VMEM((D_BRANCH // LANES, tm, LANES), F32)] * (2 * N_BRANCH),
        compiler_params=pltpu.CompilerParams(dimension_semantics=("parallel",)),
        name="out_ffn_prompt",
    )(x2d, *att, ssd2d, *weights)


def _out_ffn_sample(x2d, att2d, ssd2d, weights, final):
    m = x2d.shape[0]
    tm = m // 2
    tok = lambda w: pl.BlockSpec((tm, w), lambda i: (i, 0))
    return pl.pallas_call(
        functools.partial(_out_ffn_sample_kernel, final),
        grid=(m // tm,),
        in_specs=[tok(D_MODEL), tok(D_ATT), tok(D_INNER)] + _ffn_weight_specs(),
        out_specs=tok(D_MODEL),
        out_shape=jax.ShapeDtypeStruct((m, D_MODEL), F32),
        scratch_shapes=[pltpu.VMEM((tm, D_FF), BF16)],
        compiler_params=pltpu.CompilerParams(dimension_semantics=("parallel",)),
        name="out_ffn_sample",
    )(x2d, att2d, ssd2d, *weights)


def _pad_lanes(v, width):
    return jnp.pad(v, ((0, 0), (0, width - v.shape[1])))


def kernel(x_prompt, x_sample, cache_kv_d1, cache_kv_d4, cache_kv_d16, state_conv, state_ssm, norm_mix, w_in, conv_w, conv_b, dt_bias, a_log, d_skip, norm_ssd, w_out, norm_ffn, w_gate_up, w_down, norm_final):
    bp, tp, _ = x_prompt.shape
    bs, ts, _ = x_sample.shape
    depth = w_in.shape[0]
    caches = (cache_kv_d1, cache_kv_d4, cache_kv_d16)
    hp, hs = x_prompt.reshape(bp * tp, D_MODEL), x_sample.reshape(bs * ts, D_MODEL)
    gfin = norm_final.reshape(1, D_MODEL)
    p_new, s_new = [[] for _ in range(5)], [[] for _ in range(5)]

    for l in range(depth):
        wi = w_in[l]
        wk, wv = wi[:, D_ATT:2 * D_ATT], wi[:, 2 * D_ATT:3 * D_ATT]
        wq = wi[:, :D_ATT].astype(BF16)
        wkv = jnp.concatenate([w[:, g * D_BRANCH:(g + 1) * D_BRANCH] for g in range(N_BRANCH) for w in (wk, wv)],
                              axis=1).astype(BF16)
        o = 3 * D_ATT
        wzx = wi[:, o:o + D_INNER + CONV_DIM].astype(BF16)
        wdt = _pad_lanes(wi[:, o + D_INNER + CONV_DIM:], DT_PAD).astype(BF16)
        gmix = norm_mix[l].reshape(1, D_MODEL)
        ssd_params = (conv_w[l], conv_b[l].reshape(1, CONV_DIM),
                      _pad_lanes(dt_bias[l].reshape(1, SSD_HEADS), DT_PAD),
                      _pad_lanes(a_log[l].reshape(1, SSD_HEADS), DT_PAD),
                      jnp.repeat(d_skip[l], SSD_HEAD_DIM).reshape(1, D_INNER),
                      norm_ssd[l].reshape(1, D_INNER))
        wgu = w_gate_up[l].astype(BF16)
        ffn_w = (w_out[l].astype(BF16), norm_ffn[l].reshape(1, D_MODEL), wgu, wgu, w_down[l].astype(BF16), gfin)
        final = l == depth - 1

        qkv, kts, (z, xbc, dt) = _in_proj_prompt(hp, bp, tp, gmix, wq, wkv, wzx, wdt)
        for g, (win, dil) in enumerate(ATT_BRANCHES):
            kt = kts[g].reshape(bp, 2, HEADS_PER_BRANCH, HEAD_DIM, win)
            p_new[g].append(jnp.transpose(kt, (0, 4, 1, 2, 3)))
        assert ts >= CONV_WIDTH - 1 and LANES % (ts * _SS_BB) == 0 and bs % _SS_BB == 0
        q_s, xbc_s, new_t, zxt, dtt = _in_proj_sample(hs, gmix, wq, wzx, wkv.T, wzx.T, wdt.T)
        att = _attn_prompt(qkv, bp, tp)

        xbc3 = xbc.reshape(bp, tp, CONV_DIM)
        conv0 = jnp.zeros((bp, SUBLANES, CONV_DIM), F32)
        ssm0 = jnp.zeros((bp, SSD_HEADS, SSD_HEAD_DIM, SSD_STATE), F32)
        caches_t = [jnp.transpose(c[l], (0, 2, 3, 4, 1)).reshape(bs, KV_W, c.shape[2]) for c in caches]
        y_ssd, ssm_p, att_s, shifted = _ssd_and_sample_attn(
            xbc3, z.reshape(bp, tp, D_INNER), dt.reshape(bp, tp, DT_PAD), conv0, ssm0, ssd_params,
            q_s.reshape(bs, ts, D_ATT), new_t, caches_t)
        p_new[3].append(xbc3[:, tp - (CONV_WIDTH - 1):])
        p_new[4].append(ssm_p)
        for g in range(N_BRANCH):
            lg = shifted[g].shape[2]
            s_new[g].append(jnp.transpose(shifted[g].reshape(bs, 2, HEADS_PER_BRANCH, HEAD_DIM, lg), (0, 4, 1, 2, 3)))
        hp = _out_ffn_prompt(hp, bp, tp, att, y_ssd.reshape(bp * tp, D_INNER), ffn_w, final)

        xbc3 = xbc_s.reshape(bs, ts, CONV_DIM)
        st_tok = jnp.pad(state_conv[l], ((0, 0), (0, ts - (CONV_WIDTH - 1)), (0, 0))).reshape(bs * ts, CONV_DIM)
        pad_heads = lambda v: jnp.pad(v, (0, _HEAD_ROWS - SSD_HEADS))
        ccol = jnp.broadcast_to(jnp.concatenate([conv_w[l], conv_b[l][None]], axis=0)[:, :, None],
                                (CONV_WIDTH + 1, CONV_DIM, LANES))
        hcol = jnp.stack([pad_heads(dt_bias[l]), pad_heads(a_log[l])], axis=1)
        ycol = jnp.stack([jnp.repeat(d_skip[l], SSD_HEAD_DIM), norm_ssd[l]], axis=1)
        y_ssd, ssm_s = _ssd_sample(zxt, st_tok, dtt, state_ssm[l], ccol, hcol, ycol, ts)
        s_new[3].append(jnp.concatenate([state_conv[l], xbc3], axis=1)[:, -(CONV_WIDTH - 1):])
        s_new[4].append(ssm_s)
        hs = _out_ffn_sample(hs, att_s.reshape(bs * ts, D_ATT), y_ssd, ffn_w, final)

    y_prompt = hp.reshape(bp, tp, D_MODEL)
    y_sample = hs.reshape(bs, ts, D_MODEL)
    p_out = [jnp.stack(a) for a in p_new]
    s_out = [jnp.stack(a) for a in s_new]
    return (y_prompt, y_sample, *p_out, *s_out)
```
